```python
import jax, jax.numpy as jnp
from jax import lax
import numpy as np

D_MODEL = 1024
BATCH = 4
SEQ = 4096
DEPTH = 1
DEC_BATCH = 128
DEC_SEQ = 8
PAST_LEN = 8192
PAGE_SIZE = 128

N_HEADS = 16
HEAD_DIM = 64
KV_HEADS = 4
HPG = N_HEADS // KV_HEADS
CMP_LEN = 32
CMP_STRIDE = 16
SEL_BLOCK = 64
N_SEL = 16
WINDOW = 512
PHI_HID = 128
Q_BLOCK = 64
CONV_DIM = D_MODEL
CONV_WIDTH = 31
D_FF = 2816
N_MOD = 9
EPS = 1e-6
NEG = -1e30
FORCE = 1e4

Q_W = N_HEADS * HEAD_DIM
KVP_W = KV_HEADS * 4 * HEAD_DIM
KVW_W = KV_HEADS * 2 * HEAD_DIM
NG_W = 3 * N_HEADS
GLU_W = 2 * CONV_DIM
MG_W = 2 * D_MODEL
D_PROJ = Q_W + KVP_W + KVW_W + NG_W + GLU_W + MG_W
SPLITS = [Q_W, Q_W + KVP_W, Q_W + KVP_W + KVW_W, Q_W + KVP_W + KVW_W + NG_W,
          Q_W + KVP_W + KVW_W + NG_W + GLU_W]

kernel_name = "nsa_conformer_conv_gated_macaron_decoder_step"


def rmsnorm(x, g):
    x32 = x.astype(jnp.float32)
    y = x32 * lax.rsqrt(jnp.mean(x32 * x32, axis=-1, keepdims=True) + EPS)
    return (y * g.astype(jnp.float32)).astype(x.dtype)


def layernorm(x, g, b):
    x32 = x.astype(jnp.float32)
    xc = x32 - jnp.mean(x32, axis=-1, keepdims=True)
    y = xc * lax.rsqrt(jnp.mean(xc * xc, axis=-1, keepdims=True) + EPS)
    return (y * g.astype(jnp.float32) + b.astype(jnp.float32)).astype(x.dtype)


def swiglu(x, w_in, w_out):
    gate, up = jnp.split(x @ w_in, 2, axis=-1)
    return (jax.nn.silu(gate) * up) @ w_out


def masked_softmax(s, mask):
    s = jnp.where(mask, s, NEG)
    e = jnp.where(mask, jnp.exp(s - jnp.max(s, axis=-1, keepdims=True)), 0.0)
    return e / jnp.maximum(jnp.sum(e, axis=-1, keepdims=True), 1e-30)


def alibi_slopes():
    h = jnp.arange(1, N_HEADS + 1, dtype=jnp.float32)
    return jnp.exp2(-8.0 * h / N_HEADS).reshape(KV_HEADS, HPG)


def chunk_partials(rows, w_phi1):
    b, L = rows.shape[:2]
    n = L // CMP_STRIDE
    ch = rows[:, :n * CMP_STRIDE].reshape(b, n, CMP_STRIDE, KV_HEADS, 2, HEAD_DIM)
    w = w_phi1.reshape(2, CMP_LEN // CMP_STRIDE, CMP_STRIDE, HEAD_DIM, PHI_HID)
    return jnp.einsum('bnsgkd,kosdh->bnogkh', ch, w)


def compress_finish(part, pe_cmp, w_phi1, b_phi1, w_phi2):
    c = CMP_LEN // CMP_STRIDE
    n_cmp = part.shape[1] - c + 1
    h = b_phi1 + jnp.einsum('kld,kldh->kh', pe_cmp, w_phi1)
    for o in range(c):
        h = h + part[:, o:o + n_cmp, o]
    kvc = jnp.einsum('bngkh,khd->bngkd', jax.nn.silu(h), w_phi2)
    pos = jnp.arange(n_cmp, dtype=jnp.int32) * CMP_STRIDE + (CMP_LEN - 1)
    return kvc, pos


def cmp_to_sel(p, n_sel_blocks):
    r = SEL_BLOCK // CMP_STRIDE
    c = CMP_LEN // CMP_STRIDE
    back = max(r * n_sel_blocks - p.shape[-1], 0)
    pp = jnp.pad(p, [(0, 0)] * (p.ndim - 1) + [(c - 1, back)])
    out = pp[..., 0:r * (n_sel_blocks - 1) + 1:r]
    for o in range(1, r + c - 1):
        out = out + pp[..., o:o + r * (n_sel_blocks - 1) + 1:r]
    return out


def nsa_block(q, t, kvw, kw_pos, kvc, cmp_pos, fetch_sel, n_sel_blocks, slopes):
    f32 = jnp.float32
    scale = HEAD_DIM ** -0.5
    sl = slopes[None, None, :, :, None]
    dist_c = t[:, None] - cmp_pos[None, :]
    s = jnp.einsum('bqghd,bngd->bqghn', q, kvc[:, :, :, 0]).astype(f32) * scale \
        - sl * dist_c[None, :, None, None, :].astype(f32)
    p_c = masked_softmax(s, (dist_c >= 0)[None, :, None, None, :])
    o_c = jnp.einsum('bqghn,bngd->bqghd', p_c.astype(q.dtype), kvc[:, :, :, 1])
    imp = cmp_to_sel(p_c.sum(axis=3), n_sel_blocks)
    blk = jnp.arange(n_sel_blocks, dtype=jnp.int32)[None, :]
    cur = t // SEL_BLOCK
    valid = blk <= cur[:, None]
    forced = (blk == 0) | (blk == cur[:, None]) | (blk == cur[:, None] - 1)
    score = jnp.where(valid[None, :, None, :],
                      imp + jnp.where(forced, FORCE, 0.0)[None, :, None, :], -1.0)
    _, idx = lax.top_k(score, min(N_SEL, n_sel_blocks))
    kv_g = fetch_sel(idx)
    spos = idx[..., None] * SEL_BLOCK + jnp.arange(SEL_BLOCK, dtype=jnp.int32)
    dist_s = t[None, :, None, None, None] - spos
    mask_s = (idx <= cur[None, :, None, None])[..., None] & (dist_s >= 0)
    s = jnp.einsum('bqghd,bqgksd->bqghks', q, kv_g[..., 0, :]).astype(f32) * scale \
        - sl[..., None] * dist_s[:, :, :, None].astype(f32)
    p_s = masked_softmax(s.reshape(s.shape[:4] + (-1,)),
                         mask_s.reshape(mask_s.shape[:3] + (1, -1))).reshape(s.shape)
    o_s = jnp.einsum('bqghks,bqgksd->bqghd', p_s.astype(q.dtype), kv_g[..., 1, :])
    dist_w = t[:, None] - kw_pos[None, :]
    mask_w = (dist_w >= 0) & (dist_w <= WINDOW) & (kw_pos[None, :] >= 0)
    s = jnp.einsum('bqghd,bkgd->bqghk', q, kvw[:, :, :, 0]).astype(f32) * scale \
        - sl * dist_w[None, :, None, None, :].astype(f32)
    p_w = masked_softmax(s, mask_w[None, :, None, None, :])
    o_w = jnp.einsum('bqghk,bkgd->bqghd', p_w.astype(q.dtype), kvw[:, :, :, 1])
    return jnp.stack([o_c, o_s, o_w], axis=2)


def nsa_prompt(q, kvp, kvw, pe_cmp, w_phi1, b_phi1, w_phi2):
    b, T = q.shape[:2]
    qh = q.reshape(b, T, KV_HEADS, HPG, HEAD_DIM)
    slopes = alibi_slopes()
    kvc, cmp_pos = compress_finish(chunk_partials(kvp[:, :, :, 0:2], w_phi1),
                                   pe_cmp, w_phi1, b_phi1, w_phi2)
    ns = -(-T // SEL_BLOCK)
    sel = jnp.pad(kvp[:, :, :, 2:4], ((0, 0), (0, ns * SEL_BLOCK - T), (0, 0), (0, 0), (0, 0)))
    sel = sel.reshape(b, ns, SEL_BLOCK, KV_HEADS, 2, HEAD_DIM)
    bi = jnp.arange(b)[:, None, None, None]
    gi = jnp.arange(KV_HEADS)[None, None, :, None]

    def fetch(idx):
        return sel[bi, idx, :, gi, :, :]

    kw_pad = jnp.pad(kvw, ((0, 0), (WINDOW, 0), (0, 0), (0, 0), (0, 0)))
    nqb = T // Q_BLOCK
    q_blocks = jnp.moveaxis(qh.reshape(b, nqb, Q_BLOCK, KV_HEADS, HPG, HEAD_DIM), 1, 0)

    def step(args):
        qb, i = args
        start = i * Q_BLOCK
        t = start + jnp.arange(Q_BLOCK, dtype=jnp.int32)
        kvw_b = lax.dynamic_slice_in_dim(kw_pad, start, WINDOW + Q_BLOCK, axis=1)
        kpos = start - WINDOW + jnp.arange(WINDOW + Q_BLOCK, dtype=jnp.int32)
        return nsa_block(qb, t, kvw_b, kpos, kvc, cmp_pos, fetch, ns, slopes)

    o = lax.map(step, (q_blocks, jnp.arange(nqb, dtype=jnp.int32)))
    return jnp.moveaxis(o, 0, 1).reshape(b, T, 3, N_HEADS, HEAD_DIM)


def nsa_sample(q, kvp, kvw, cache_kv, page_table, win_buf, layer, pe_cmp, w_phi1, b_phi1, w_phi2):
    db, ds = q.shape[:2]
    past = page_table.shape[1] * PAGE_SIZE
    wb = win_buf.shape[1]
    qh = q.reshape(db, ds, KV_HEADS, HPG, HEAD_DIM)
    slopes = alibi_slopes()
    past_c = cache_kv[page_table, :, layer, :, 0:2].reshape(db, past, KV_HEADS, 2, HEAD_DIM)
    part = jnp.concatenate([chunk_partials(past_c, w_phi1),
                            chunk_partials(kvp[:, :, :, 0:2], w_phi1)], axis=1)
    kvc, cmp_pos = compress_finish(part, pe_cmp, w_phi1, b_phi1, w_phi2)
    ns = -(-(past + ds) // SEL_BLOCK)
    npb = past // SEL_BLOCK
    nnb = ns - npb
    new_sel = jnp.pad(kvp[:, :, :, 2:4], ((0, 0), (0, nnb * SEL_BLOCK - ds), (0, 0), (0, 0), (0, 0)))
    new_sel = new_sel.reshape(db, nnb, SEL_BLOCK, KV_HEADS, 2, HEAD_DIM)
    bpp = PAGE_SIZE // SEL_BLOCK
    bi = jnp.arange(db)[:, None, None, None]
    gi = jnp.arange(KV_HEADS)[None, None, :, None]

    def fetch(idx):
        jp = jnp.minimum(idx, npb - 1)
        phys = page_table[bi, jp // bpp]
        rows = (jp % bpp)[..., None] * SEL_BLOCK + jnp.arange(SEL_BLOCK, dtype=jnp.int32)
        from_past = cache_kv[phys[..., None], rows, layer, gi[..., None], 2:4, :]
        from_new = new_sel[bi, jnp.clip(idx - npb, 0, nnb - 1), :, gi, :, :]
        return jnp.where((idx < npb)[..., None, None, None], from_past, from_new)

    kvw_all = jnp.concatenate([win_buf.astype(kvw.dtype), kvw], axis=1)
    kpos = past - wb + jnp.arange(wb + ds, dtype=jnp.int32)
    q_tok = jnp.moveaxis(qh, 1, 0)[:, :, None]

    def step(args):
        qt, j = args
        return nsa_block(qt, (past + j)[None], kvw_all, kpos, kvc, cmp_pos, fetch, ns, slopes)

    o = lax.map(step, (q_tok, jnp.arange(ds, dtype=jnp.int32)))
    return jnp.moveaxis(o[:, :, 0], 0, 1).reshape(db, ds, 3, N_HEADS, HEAD_DIM)


def conv_module(glu_in, buf, w_dw, b_dw, g_ln, b_ln):
    a, g = jnp.split(glu_in, 2, axis=-1)
    u = a * jax.nn.sigmoid(g)
    ext = jnp.concatenate([buf.astype(u.dtype), u], axis=1)
    y = lax.conv_general_dilated(ext, w_dw[:, None, :].astype(u.dtype), (1,), 'VALID',
                                 dimension_numbers=('NWC', 'WIO', 'NWC'),
                                 feature_group_count=CONV_DIM) + b_dw
    return jax.nn.silu(layernorm(y, g_ln, b_ln)), ext[:, -(CONV_WIDTH - 1):]


def layer_forward(x, c, attn_fn, conv_buf, w_ada, b_ada, g_n1, g_n2, g_n3, w_ff1_in, w_ff1_out,
                  w_ff2_in, w_ff2_out, w_in, w_dw, b_dw, g_cln, b_cln, w_br_attn, w_br_conv, w_out):
    b, T, _ = x.shape
    mod = (jax.nn.silu(c) @ w_ada + b_ada)[:, None, :]
    sh1, sc1, gt1, sh2, sc2, gt2, sh3, sc3, gt3 = jnp.split(mod, N_MOD, axis=-1)
    h = x + gt1 * (0.5 * swiglu(rmsnorm(x, g_n1) * (1 + sc1) + sh1, w_ff1_in, w_ff1_out))
    n = rmsnorm(h, g_n2) * (1 + sc2) + sh2
    q, kvp, kvw, ng, glu, mg = jnp.split(n @ w_in, SPLITS, axis=-1)
    kvp = kvp.reshape(b, T, KV_HEADS, 4, HEAD_DIM)
    kvw = kvw.reshape(b, T, KV_HEADS, 2, HEAD_DIM)
    o3 = attn_fn(q, kvp, kvw)
    attn = jnp.einsum('btrh,btrhd->bthd', jax.nn.sigmoid(ng).reshape(b, T, 3, N_HEADS),
                      o3.astype(x.dtype)).reshape(b, T, Q_W)
    conv, conv_state = conv_module(glu, conv_buf, w_dw, b_dw, g_cln, b_cln)
    ga, gc = jnp.split(mg, 2, axis=-1)
    m = jax.nn.sigmoid(ga) * (attn @ w_br_attn) + jax.nn.sigmoid(gc) * (conv @ w_br_conv)
    h = h + gt2 * (m @ w_out)
    h = h + gt3 * (0.5 * swiglu(rmsnorm(h, g_n3) * (1 + sc3) + sh3, w_ff2_in, w_ff2_out))
    return h, kvp, kvw, conv_state


def setup_inputs(seed: int = 0) -> dict:
    key = jax.random.key(seed)
    ks = jax.random.split(key, 40)
    f32 = jnp.float32
    n_pages = PAST_LEN // PAGE_SIZE
    n_pool = (5 * DEC_BATCH * n_pages + 3) // 4
    wb = min(WINDOW, PAST_LEN)

    def nrm(k, shape, s=1.0):
        return s * jax.random.normal(k, shape, f32)

    def gain(k, shape):
        return 1.0 + 0.05 * jax.random.normal(k, shape, f32)

    page_table = jax.random.permutation(ks[7], n_pool)[:DEC_BATCH * n_pages]
    page_table = page_table.reshape(DEC_BATCH, n_pages).astype(jnp.int32)
    return {
        "x_prompt": nrm(ks[0], (BATCH, SEQ, D_MODEL)),
        "x_sample": nrm(ks[1], (DEC_BATCH, DEC_SEQ, D_MODEL)),
        "c_prompt": nrm(ks[2], (BATCH, D_MODEL)),
        "c_sample": nrm(ks[3], (DEC_BATCH, D_MODEL)),
        "cache_kv": nrm(ks[4], (n_pool, PAGE_SIZE, DEPTH, KV_HEADS, 4, HEAD_DIM)),
        "state_win": nrm(ks[5], (DEC_BATCH, wb, DEPTH, KV_HEADS, 2, HEAD_DIM)),
        "state_conv": nrm(ks[6], (DEC_BATCH, CONV_WIDTH - 1, DEPTH, CONV_DIM)),
        "page_table": page_table,
        "w_ada": nrm(ks[8], (DEPTH, D_MODEL, N_MOD * D_MODEL), D_MODEL ** -0.5),
        "b_ada": nrm(ks[9], (DEPTH, N_MOD * D_MODEL), 0.02),
        "g_norm1": gain(ks[10], (DEPTH, D_MODEL)),
        "g_norm2": gain(ks[11], (DEPTH, D_MODEL)),
        "g_norm3": gain(ks[12], (DEPTH, D_MODEL)),
        "g_final": gain(ks[13], (D_MODEL,)),
        "w_ff1_in": nrm(ks[14], (DEPTH, D_MODEL, 2 * D_FF), D_MODEL ** -0.5),
        "w_ff1_out": nrm(ks[15], (DEPTH, D_FF, D_MODEL), D_FF ** -0.5),
        "w_ff2_in": nrm(ks[16], (DEPTH, D_MODEL, 2 * D_FF), D_MODEL ** -0.5),
        "w_ff2_out": nrm(ks[17], (DEPTH, D_FF, D_MODEL), D_FF ** -0.5),
        "w_in": nrm(ks[18], (DEPTH, D_MODEL, D_PROJ), D_MODEL ** -0.5),
        "pe_cmp": nrm(ks[19], (DEPTH, 2, CMP_LEN, HEAD_DIM), 0.1),
        "w_phi1": nrm(ks[20], (DEPTH, 2, CMP_LEN, HEAD_DIM, PHI_HID), (CMP_LEN * HEAD_DIM) ** -0.5),
        "b_phi1": nrm(ks[21], (DEPTH, 2, PHI_HID), 0.02),
        "w_phi2": nrm(ks[22], (DEPTH, 2, PHI_HID, HEAD_DIM), PHI_HID ** -0.5),
        "w_dw": nrm(ks[23], (DEPTH, CONV_WIDTH, CONV_DIM), CONV_WIDTH ** -0.5),
        "b_dw": nrm(ks[24], (DEPTH, CONV_DIM), 0.02),
        "g_conv_ln": gain(ks[25], (DEPTH, CONV_DIM)),
        "b_conv_ln": nrm(ks[26], (DEPTH, CONV_DIM), 0.02),
        "w_br_attn": nrm(ks[27], (DEPTH, Q_W, D_MODEL), Q_W ** -0.5),
        "w_br_conv": nrm(ks[28], (DEPTH, CONV_DIM, D_MODEL), CONV_DIM ** -0.5),
        "w_out": nrm(ks[29], (DEPTH, D_MODEL, D_MODEL), D_MODEL ** -0.5),
    }


def reference(x_prompt, x_sample, c_prompt, c_sample, cache_kv, state_win, state_conv, page_table,
              w_ada, b_ada, g_norm1, g_norm2, g_norm3, g_final, w_ff1_in, w_ff1_out, w_ff2_in,
              w_ff2_out, w_in, pe_cmp, w_phi1, b_phi1, w_phi2, w_dw, b_dw, g_conv_ln, b_conv_ln,
              w_br_attn, w_br_conv, w_out):
    wb = state_win.shape[1]
    hp, hs = x_prompt, x_sample
    kv_p, kv_s, win_p, win_s, conv_p, conv_s = [], [], [], [], [], []
    for l in range(DEPTH):
        shared = (w_ada[l], b_ada[l], g_norm1[l], g_norm2[l], g_norm3[l], w_ff1_in[l], w_ff1_out[l],
                  w_ff2_in[l], w_ff2_out[l], w_in[l], w_dw[l], b_dw[l], g_conv_ln[l], b_conv_ln[l],
                  w_br_attn[l], w_br_conv[l], w_out[l])
        cmp_w = (pe_cmp[l], w_phi1[l], b_phi1[l], w_phi2[l])
        zero_buf = jnp.zeros((hp.shape[0], CONV_WIDTH - 1, CONV_DIM), hp.dtype)
        hp, kvp, kvw, cst = layer_forward(
            hp, c_prompt, lambda q, a, w: nsa_prompt(q, a, w, *cmp_w), zero_buf, *shared)
        kv_p.append(kvp)
        pad = max(wb - kvw.shape[1], 0)
        win_p.append(jnp.pad(kvw, ((0, 0), (pad, 0), (0, 0), (0, 0), (0, 0)))[:, -wb:])
        conv_p.append(cst)
        buf_w = state_win[:, :, l]
        hs, kvp, kvw, cst = layer_forward(
            hs, c_sample,
            lambda q, a, w: nsa_sample(q, a, w, cache_kv, page_table, buf_w, l, *cmp_w),
            state_conv[:, :, l], *shared)
        kv_s.append(kvp)
        win_s.append(jnp.concatenate([buf_w.astype(kvw.dtype), kvw], axis=1)[:, -wb:])
        conv_s.append(cst)
    y_prompt = rmsnorm(hp, g_final)
    y_sample = rmsnorm(hs, g_final)
    return (y_prompt, y_sample, jnp.stack(kv_p, axis=2), jnp.stack(kv_s, axis=2),
            jnp.stack(win_p, axis=2), jnp.stack(win_s, axis=2),
            jnp.stack(conv_p, axis=2), jnp.stack(conv_s, axis=2))
```

```python
import functools

import numpy as np
import jax
import jax.numpy as jnp
from jax import lax
from jax.experimental import pallas as pl
from jax.experimental.pallas import tpu as pltpu

F32 = jnp.float32
BF = jnp.bfloat16

N_HEADS = 16
HEAD_DIM = 64
KV_HEADS = 4
HPG = N_HEADS // KV_HEADS
CMP_LEN = 32
CMP_STRIDE = 16
SEL_BLOCK = 64
N_SEL = 16
WINDOW = 512
PHI_HID = 128
CONV_WIDTH = 31
N_MOD = 9
EPS = 1e-6
NEG = -1e30
FORCE = 1e4
PAGE_SIZE = 128

LANES = 128
KV_LANES = 2 * HEAD_DIM
Q_LANES = HPG * HEAD_DIM
N_GATE = 3 * HPG
VMEM_LIMIT = 56 * 1024 * 1024


def _cparams(sem):
    return pltpu.CompilerParams(dimension_semantics=sem, vmem_limit_bytes=VMEM_LIMIT)


def _const_spec(shape):
    nd = len(shape)
    return pl.BlockSpec(shape, lambda *_: (0,) * nd, pipeline_mode=pl.Buffered(1))


def _tok(ref):
    tb, rb, w = ref.shape
    return ref[0] if tb == 1 else ref[...].reshape(tb * rb, w)


def _modrow(ref, rb):
    tb, _, w = ref.shape
    if tb == 1:
        return ref[0]
    return jnp.broadcast_to(ref[...], (tb, rb, w)).reshape(tb * rb, w)


def _put(ref, val):
    tb, rb, w = ref.shape
    if tb == 1:
        ref[0] = val.astype(ref.dtype)
    else:
        ref[...] = val.reshape(tb, rb, w).astype(ref.dtype)


def _rms(x, g):
    return x * lax.rsqrt(jnp.mean(x * x, axis=-1, keepdims=True) + EPS) * g


def _silu(x):
    return x * jax.nn.sigmoid(x)


def _mm(a, b):
    return jnp.dot(a, b, preferred_element_type=F32)


def _mm_t(a, b):
    return lax.dot_general(a, b, (((1,), (1,)), ((), ())), preferred_element_type=F32)


def _tok_spec(tb, rb, w):
    return pl.BlockSpec((tb, rb, w), lambda i, j: (i, j, 0))


def _mod_spec(tb, d, col):
    return pl.BlockSpec((tb, 1, d), lambda i, j: (i, 0, col))


def _mod_kernel(c_ref, w_ref, b_ref, o_ref):
    a = _silu(c_ref[...]).astype(BF)
    o_ref[...] = _mm(a, w_ref[...].astype(BF)) + b_ref[...]


def _mod_call(c, w, b):
    m, d = c.shape
    n = w.shape[1]
    tn = d
    return pl.pallas_call(
        _mod_kernel,
        grid=(n // tn,),
        in_specs=[pl.BlockSpec((m, d), lambda j: (0, 0)),
                  pl.BlockSpec((d, tn), lambda j: (0, j)),
                  pl.BlockSpec((1, tn), lambda j: (0, j))],
        out_specs=pl.BlockSpec((m, tn), lambda j: (0, j)),
        out_shape=jax.ShapeDtypeStruct((m, n), F32),
        compiler_params=_cparams(("arbitrary",)),
    )(c, w, b.reshape(1, n))


def _ffn_kernel(*refs, d_ff, fc, final):
    if final:
        x_ref, sh_ref, sc_ref, gt_ref, g_ref, wi_ref, wo_ref, gf_ref, h_ref, y_ref = refs
    else:
        x_ref, sh_ref, sc_ref, gt_ref, g_ref, wi_ref, wo_ref, h_ref = refs
    rb = x_ref.shape[1]
    x = _tok(x_ref)
    n = (_rms(x, g_ref[...]) * (1.0 + _modrow(sc_ref, rb)) + _modrow(sh_ref, rb)).astype(BF)
    acc = jnp.zeros(x.shape, F32)
    for c in range(d_ff // fc):
        gate = _mm(n, wi_ref[:, c * fc:(c + 1) * fc])
        up = _mm(n, wi_ref[:, d_ff + c * fc:d_ff + (c + 1) * fc])
        acc = acc + _mm((_silu(gate) * up).astype(BF), wo_ref[c * fc:(c + 1) * fc, :])
    h = x + _modrow(gt_ref, rb) * (0.5 * acc)
    _put(h_ref, h)
    if final:
        _put(y_ref, _rms(h, gf_ref[...]))


def _ffn_call(x, mod, col0, g, wi, wo, tb, rb, gf=None):
    nb, r, d = x.shape
    d_ff = wo.shape[0]
    fc = d_ff // 2
    final = gf is not None
    ins = [x, mod, mod, mod, g.reshape(1, d), wi, wo]
    specs = [_tok_spec(tb, rb, d), _mod_spec(tb, d, col0), _mod_spec(tb, d, col0 + 1),
             _mod_spec(tb, d, col0 + 2), _const_spec((1, d)), _const_spec(wi.shape), _const_spec(wo.shape)]
    out_shape = [jax.ShapeDtypeStruct(x.shape, F32)]
    out_specs = [_tok_spec(tb, rb, d)]
    if final:
        ins.append(gf.reshape(1, d))
        specs.append(_const_spec((1, d)))
        out_shape.append(jax.ShapeDtypeStruct(x.shape, F32))
        out_specs.append(_tok_spec(tb, rb, d))
    return pl.pallas_call(
        functools.partial(_ffn_kernel, d_ff=d_ff, fc=fc, final=final),
        grid=(nb // tb, r // rb),
        in_specs=specs, out_specs=out_specs, out_shape=out_shape,
        compiler_params=_cparams(("arbitrary", "arbitrary")),
    )(*ins)


def _proj_kernel(h_ref, sh_ref, sc_ref, g_ref, wq_ref, wkvp_ref, wkvw_ref, wng_ref, wglu_ref, wmg_ref,
                 q_ref, kvp_ref, kvpb_ref, kvw_ref, kvwb_ref, gates_ref, u_ref, sga_ref, sgc_ref):
    tb, rb, d = h_ref.shape
    n = (_rms(_tok(h_ref), g_ref[...]) * (1.0 + _modrow(sc_ref, rb)) + _modrow(sh_ref, rb)).astype(BF)
    _put(q_ref, _mm(n, wq_ref[...]))
    kvp = _mm(n, wkvp_ref[...])
    _put(kvp_ref, kvp)
    _put(kvpb_ref, kvp)
    kvw = _mm(n, wkvw_ref[...])
    _put(kvw_ref, kvw)
    _put(kvwb_ref, kvw)
    ng = jax.nn.sigmoid(_mm(n, wng_ref[...]))
    for g in range(KV_HEADS):
        gates_ref[g] = ng[:, g * N_GATE:(g + 1) * N_GATE].reshape(tb, rb, N_GATE)
    glu = _mm(n, wglu_ref[...])
    c = glu.shape[1] // 2
    _put(u_ref, glu[:, :c] * jax.nn.sigmoid(glu[:, c:]))
    mg = jax.nn.sigmoid(_mm(n, wmg_ref[...]))
    _put(sga_ref, mg[:, :d])
    _put(sgc_ref, mg[:, d:])


def _proj_call(h, mod, g, ws, tb, rb):
    nb, r, d = h.shape
    wq, wkvp, wkvw, wng, wglu, wmg = ws
    c = wglu.shape[1] // 2
    widths = [(wq.shape[1], BF), (wkvp.shape[1], F32), (wkvp.shape[1], BF), (wkvw.shape[1], F32),
              (wkvw.shape[1], BF)]
    out_shape = [jax.ShapeDtypeStruct((nb, r, w), dt) for w, dt in widths]
    out_specs = [_tok_spec(tb, rb, w) for w, _ in widths]
    out_shape.append(jax.ShapeDtypeStruct((KV_HEADS, nb, r, N_GATE), F32))
    out_specs.append(pl.BlockSpec((KV_HEADS, tb, rb, N_GATE), lambda i, j: (0, i, j, 0)))
    for w in (c, d, d):
        out_shape.append(jax.ShapeDtypeStruct((nb, r, w), F32))
        out_specs.append(_tok_spec(tb, rb, w))
    return pl.pallas_call(
        _proj_kernel,
        grid=(nb // tb, r // rb),
        in_specs=[_tok_spec(tb, rb, d), _mod_spec(tb, d, 3), _mod_spec(tb, d, 4), _const_spec((1, d))]
                 + [_const_spec(w.shape) for w in ws],
        out_specs=out_specs, out_shape=out_shape,
        compiler_params=_cparams(("arbitrary", "arbitrary")),
    )(h, mod, mod, g.reshape(1, d), *ws)


CONV_HIST = 32


def _conv_kernel(u_ref, hist_ref, w_ref, b_ref, g_ref, bl_ref, o_ref, ext_ref, *, zero_first, rs):
    tb, rb, c = u_ref.shape
    hr = hist_ref.shape[1]
    hist = hist_ref[...]
    if zero_first:
        hist = jnp.where(pl.program_id(1) > 0, hist, 0.0)
    ext_ref[:, CONV_HIST - hr:CONV_HIST, :] = hist
    ext_ref[:, CONV_HIST:CONV_HIST + rb, :] = u_ref[...]
    off = CONV_HIST - (CONV_WIDTH - 1)
    w = w_ref[...]
    for s in range(tb):
        for r0 in range(0, rb, rs):
            acc = jnp.broadcast_to(b_ref[...], (rs, c))
            for k in range(CONV_WIDTH):
                acc = acc + w[k:k + 1, :] * ext_ref[s, r0 + k + off:r0 + k + off + rs, :]
            xc = acc - jnp.mean(acc, axis=-1, keepdims=True)
            y = xc * lax.rsqrt(jnp.mean(xc * xc, axis=-1, keepdims=True) + EPS) * g_ref[...] + bl_ref[...]
            o_ref[s, r0:r0 + rs, :] = _silu(y).astype(o_ref.dtype)


def _conv_call(u, hist, w, b, g, bl, tb, rb, rs):
    nb, r, c = u.shape
    row = lambda v: v.reshape(1, c)
    if hist is None:
        hist_arr, hr = u, CONV_HIST
        hspec = pl.BlockSpec((tb, hr, c), lambda i, j: (i, jnp.maximum(j * (rb // CONV_HIST) - 1, 0), 0))
    else:
        hist_arr, hr = hist, hist.shape[1]
        hspec = pl.BlockSpec((tb, hr, c), lambda i, j: (i, 0, 0))
    return pl.pallas_call(
        functools.partial(_conv_kernel, zero_first=hist is None, rs=rs),
        grid=(nb // tb, r // rb),
        in_specs=[_tok_spec(tb, rb, c), hspec, _const_spec(w.shape)] + [_const_spec((1, c))] * 3,
        out_specs=_tok_spec(tb, rb, c),
        out_shape=jax.ShapeDtypeStruct(u.shape, BF),
        scratch_shapes=[pltpu.VMEM((tb, CONV_HIST + rb, c), F32)],
        compiler_params=_cparams(("arbitrary", "arbitrary")),
    )(u, hist_arr, w, row(b), row(g), row(bl))


def _merge_kernel(h_ref, attn_ref, conv_ref, sga_ref, sgc_ref, gt_ref, wa_ref, wc_ref, wo_ref, o_ref):
    rb = h_ref.shape[1]
    m = _tok(sga_ref) * _mm(_tok(attn_ref), wa_ref[...]) + _tok(sgc_ref) * _mm(_tok(conv_ref), wc_ref[...])
    _put(o_ref, _tok(h_ref) + _modrow(gt_ref, rb) * _mm(m.astype(BF), wo_ref[...]))


def _merge_call(h, attn, conv, sga, sgc, mod, wa, wc, wo, tb, rb):
    nb, r, d = h.shape
    return pl.pallas_call(
        _merge_kernel,
        grid=(nb // tb, r // rb),
        in_specs=[_tok_spec(tb, rb, d), _tok_spec(tb, rb, attn.shape[2]), _tok_spec(tb, rb, conv.shape[2]),
                  _tok_spec(tb, rb, d), _tok_spec(tb, rb, d), _mod_spec(tb, d, 5),
                  _const_spec(wa.shape), _const_spec(wc.shape), _const_spec(wo.shape)],
        out_specs=_tok_spec(tb, rb, d),
        out_shape=jax.ShapeDtypeStruct(h.shape, F32),
        compiler_params=_cparams(("arbitrary", "arbitrary")),
    )(h, attn, conv, sga, sgc, mod, wa, wc, wo)


def _cmp_weights(w_phi1, w_phi2):
    c = CMP_LEN // CMP_STRIDE
    w = w_phi1.reshape(2, c, CMP_STRIDE, HEAD_DIM, PHI_HID)
    eye = jnp.eye(2, dtype=w.dtype)
    wbig = jnp.einsum('kosdh,kc->skdcoh', w, eye).reshape(CMP_STRIDE * KV_LANES, 2 * c * PHI_HID)
    w2 = jnp.einsum('khd,kc->khcd', w_phi2, eye).reshape(2 * PHI_HID, KV_LANES)
    return wbig.astype(BF), w2.astype(BF)


def _cmp_bias_kernel(pe_ref, wbig_ref, b_ref, o_ref):
    res = _mm(pe_ref[...].astype(BF), wbig_ref[...])
    hs = [b_ref[:, k * PHI_HID:(k + 1) * PHI_HID]
          + res[0:1, 2 * k * PHI_HID:(2 * k + 1) * PHI_HID]
          + res[1:2, (2 * k + 1) * PHI_HID:(2 * k + 2) * PHI_HID] for k in range(2)]
    o_ref[...] = jnp.concatenate(hs, axis=1)


def _cmp_bias_call(pe_cmp, wbig, b_phi1):
    c = CMP_LEN // CMP_STRIDE
    pe = pe_cmp.reshape(2, c, CMP_STRIDE, HEAD_DIM).transpose(1, 2, 0, 3).reshape(c, CMP_STRIDE * KV_LANES)
    pe = jnp.concatenate([pe, jnp.zeros((8 - c, pe.shape[1]), pe.dtype)], axis=0)
    return pl.pallas_call(
        _cmp_bias_kernel,
        out_shape=jax.ShapeDtypeStruct((1, 2 * PHI_HID), F32),
        compiler_params=pltpu.CompilerParams(vmem_limit_bytes=VMEM_LIMIT),
    )(pe, wbig, b_phi1.reshape(1, 2 * PHI_HID))


def _compress(load_rows, wbig_ref, hb_ref, w2_ref):
    lhs = jnp.concatenate([load_rows(s).astype(BF) for s in range(CMP_STRIDE)], axis=1)
    part = _mm(lhs, wbig_ref[...])
    n = part.shape[0]
    nxt = pltpu.roll(part, n - 1, 0)
    hb = hb_ref[...]
    hs = [hb[:, k * PHI_HID:(k + 1) * PHI_HID]
          + part[:, 2 * k * PHI_HID:(2 * k + 1) * PHI_HID]
          + nxt[:, (2 * k + 1) * PHI_HID:(2 * k + 2) * PHI_HID] for k in range(2)]
    h = jnp.concatenate(hs, axis=1)
    return _mm(_silu(h).astype(BF), w2_ref[...])


def _stack_q(q, r):
    q = q.astype(F32)
    qs = jnp.concatenate([q[:, h * HEAD_DIM:(h + 1) * HEAD_DIM] for h in range(HPG)], axis=0)
    return jnp.concatenate([qs, jnp.zeros_like(qs)], axis=1).astype(BF)


def _row_info(r, g, t0):
    row = lax.broadcasted_iota(jnp.int32, (HPG * r, 1), 0)
    t = t0 + (row & (r - 1))
    head = g * HPG + lax.shift_right_logical(row, r.bit_length() - 1)
    slope = jnp.exp2(-(8.0 / N_HEADS) * (head + 1).astype(F32))
    return t, slope


def _msoftmax(s, mask):
    s = jnp.where(mask, s, NEG)
    e = jnp.where(mask, jnp.exp(s - jnp.max(s, axis=-1, keepdims=True)), 0.0)
    return e / jnp.maximum(jnp.sum(e, axis=-1, keepdims=True), 1e-30)


def _topk_mask(xt, k):
    nb = xt.shape[0]
    rowf = lax.broadcasted_iota(jnp.int32, xt.shape, 0).astype(F32)

    def body(_, carry):
        x, sel = carry
        m = jnp.max(x, axis=0, keepdims=True)
        idx = jnp.min(jnp.where(x == m, rowf, float(nb)), axis=0, keepdims=True)
        pick = rowf == idx
        return jnp.where(pick, -3.0, x), jnp.where(pick, 1.0, sel)

    _, sel = lax.fori_loop(0, k, body, (xt, jnp.zeros_like(xt)))
    return sel


def _cmp_select(qs, kvc, t, slope, r, n_cmp, ns, m_ref):
    nc = kvc.shape[0]
    nbp = m_ref.shape[1]
    ci = lax.broadcasted_iota(jnp.int32, (1, nc), 1)
    dist = t - (ci * CMP_STRIDE + (CMP_LEN - 1))
    mask = (dist >= 0) & (ci < n_cmp)
    p = _msoftmax(_mm_t(qs, kvc) - slope * dist.astype(F32), mask)
    o_c = _mm(p.astype(BF), kvc)
    psum = p[0:r] + p[r:2 * r] + p[2 * r:3 * r] + p[3 * r:4 * r]
    imp = jnp.dot(psum, m_ref[...], precision=lax.Precision.HIGHEST, preferred_element_type=F32)
    blk = lax.broadcasted_iota(jnp.int32, (1, nbp), 1)
    cur = lax.shift_right_logical(t[0:r], SEL_BLOCK.bit_length() - 1)
    forced = (blk == 0) | (blk == cur) | (blk == cur - 1)
    score = jnp.where(blk <= cur, imp + jnp.where(forced, FORCE, 0.0), -1.0)
    score = jnp.where(blk < ns, score, -2.0)
    if r < LANES:
        score = jnp.concatenate([score, jnp.full((LANES - r, nbp), -2.0, F32)], axis=0)
    sel = _topk_mask(score.T, N_SEL).T
    return o_c, sel[0:r]


def _gate_col(gt, branch):
    return jnp.concatenate([gt[:, branch * HPG + h:branch * HPG + h + 1] for h in range(HPG)], axis=0)


def _combine(gt, o_c, o_s, o_w, r):
    o = _gate_col(gt, 0) * o_c + _gate_col(gt, 1) * o_s + _gate_col(gt, 2) * o_w
    return jnp.concatenate([o[h * r:(h + 1) * r, HEAD_DIM:KV_LANES] for h in range(HPG)], axis=1)


def _sel_matrices(nc, n_cmp, nbp, ns, key_blocks):
    i = np.arange(nc)[:, None]
    j = np.arange(nbp)[None, :]
    r = SEL_BLOCK // CMP_STRIDE
    c = CMP_LEN // CMP_STRIDE
    m = (i >= r * j - (c - 1)) & (i <= r * j + r - 1) & (i < n_cmp) & (j < ns)
    e = np.arange(nbp)[:, None] == key_blocks[None, :]
    return jnp.asarray(m, F32), jnp.asarray(e, BF)


def _nsa_prompt_kernel(q_ref, kc_ref, ks_ref, kw_ref, gt_ref, wbig_ref, hb_ref, w2_ref, e_ref, m_ref,
                       o_ref, kvc_ref, *, tq, t_len, ck):
    g = pl.program_id(1)
    i = pl.program_id(2)
    n = t_len // CMP_STRIDE

    @pl.when(i == 0)
    def _():
        kvc = _compress(lambda s: kc_ref[0, pl.ds(s, n, stride=CMP_STRIDE), :], wbig_ref, hb_ref, w2_ref)
        kvc_ref[...] = kvc.astype(BF)

    start = i * tq
    qs = _stack_q(q_ref[0], tq)
    t, slope = _row_info(tq, g, start)
    o_c, sel = _cmp_select(qs, kvc_ref[...], t, slope, tq, n - 1, t_len // SEL_BLOCK, m_ref)

    lo = pl.multiple_of(jnp.maximum(start - WINDOW, 0), tq)
    kw = kw_ref[0, pl.ds(lo, WINDOW + tq), :]
    dist = t - (lo + lax.broadcasted_iota(jnp.int32, (1, WINDOW + tq), 1))
    p = _msoftmax(_mm_t(qs, kw) - slope * dist.astype(F32), (dist >= 0) & (dist <= WINDOW))
    o_w = _mm(p.astype(BF), kw)

    selb = sel.astype(BF)

    def body(c, carry):
        m, l, acc = carry
        k0 = pl.multiple_of(c * ck, ck)
        kv = ks_ref[0, pl.ds(k0, ck), :]
        ex = _mm(selb, e_ref[:, pl.ds(k0, ck)])
        dist = t - (k0 + lax.broadcasted_iota(jnp.int32, (1, ck), 1))
        mask = (jnp.concatenate([ex] * HPG, axis=0) > 0.5) & (dist >= 0)
        s = jnp.where(mask, _mm_t(qs, kv) - slope * dist.astype(F32), NEG)
        m_new = jnp.maximum(m, jnp.max(s, axis=-1, keepdims=True))
        alpha = jnp.exp(m - m_new)
        e = jnp.where(mask, jnp.exp(s - m_new), 0.0)
        l = alpha * l + jnp.sum(e, axis=-1, keepdims=True)
        acc = alpha * acc + _mm(e.astype(BF), kv)
        return m_new, l, acc

    rows = HPG * tq
    init = (jnp.full((rows, 1), NEG, F32), jnp.zeros((rows, 1), F32), jnp.zeros((rows, KV_LANES), F32))
    _, l, acc = lax.fori_loop(0, (start + tq - 1) // ck + 1, body, init)
    o_s = acc / jnp.maximum(l, 1e-30)

    o_ref[0] = _combine(gt_ref[0, 0], o_c, o_s, o_w, tq).astype(o_ref.dtype)


def _nsa_prompt_call(q, kvp, kvpb, kvwb, gates, wbig, hb, w2, tq=128, ck=512):
    b, t_len, _ = q.shape
    assert t_len % ck == 0 and t_len >= WINDOW + tq and t_len // SEL_BLOCK >= N_SEL
    nc = t_len // CMP_STRIDE
    ns = t_len // SEL_BLOCK
    nbp = -(-ns // LANES) * LANES
    m_mat, e_mat = _sel_matrices(nc, nc - 1, nbp, ns, np.arange(t_len) // SEL_BLOCK)
    return pl.pallas_call(
        functools.partial(_nsa_prompt_kernel, tq=tq, t_len=t_len, ck=ck),
        grid=(b, KV_HEADS, t_len // tq),
        in_specs=[pl.BlockSpec((1, tq, Q_LANES), lambda b, g, i: (b, i, g)),
                  pl.BlockSpec((1, t_len, KV_LANES), lambda b, g, i: (b, 0, 2 * g)),
                  pl.BlockSpec((1, t_len, KV_LANES), lambda b, g, i: (b, 0, 2 * g + 1)),
                  pl.BlockSpec((1, t_len, KV_LANES), lambda b, g, i: (b, 0, g)),
                  pl.BlockSpec((1, 1, tq, N_GATE), lambda b, g, i: (g, b, i, 0)),
                  _const_spec(wbig.shape), _const_spec(hb.shape), _const_spec(w2.shape),
                  _const_spec(e_mat.shape), _const_spec(m_mat.shape)],
        out_specs=pl.BlockSpec((1, tq, Q_LANES), lambda b, g, i: (b, i, g)),
        out_shape=jax.ShapeDtypeStruct(q.shape, BF),
        scratch_shapes=[pltpu.VMEM((nc, KV_LANES), BF)],
        compiler_params=_cparams(("arbitrary", "arbitrary", "arbitrary")),
    )(q, kvp, kvpb, kvwb, gates, wbig, hb, w2, e_mat, m_mat)


NEW_PAD = 128


def _nsa_sample_kernel(pt_ref, q_ref, kvpn_ref, kvwn_ref, win_ref, gt_ref, cache_ref, wbig_ref, hb_ref,
                       w2_ref, e_ref, m_ref, o_ref, cmp_ref, sel_ref, kvs_ref, kww_ref, sem, *, past, ds, wb):
    b = pl.program_id(0)
    g = pl.program_id(1)
    n_pages = past // PAGE_SIZE
    step = b * KV_HEADS + g
    total = pl.num_programs(0) * KV_HEADS
    slot = step % 2

    def page_copies(bb, gg, p, sl):
        row = pl.multiple_of(p * PAGE_SIZE, PAGE_SIZE)
        page = pt_ref[bb, p]
        return [pltpu.make_async_copy(
            cache_ref.at[page, :, pl.ds(pl.multiple_of((2 * gg + half) * KV_LANES, KV_LANES), KV_LANES)],
            dst.at[sl, pl.ds(row, PAGE_SIZE), :], sem.at[sl]) for half, dst in enumerate((cmp_ref, sel_ref))]

    def fetch(st, sl):
        bb = st // KV_HEADS
        gg = st % KV_HEADS

        def body(p, carry):
            for cp in page_copies(bb, gg, p, sl):
                cp.start()
            return carry

        lax.fori_loop(0, n_pages, body, 0)

    @pl.when(step == 0)
    def _():
        fetch(step, slot)

    @pl.when(step + 1 < total)
    def _():
        fetch(step + 1, 1 - slot)

    def wait_body(p, carry):
        for cp in page_copies(b, g, p, slot):
            cp.wait()
        return carry

    lax.fori_loop(0, n_pages, wait_body, 0)

    n = past // CMP_STRIDE
    kvc = _compress(lambda s: cmp_ref[slot, pl.ds(s, n, stride=CMP_STRIDE), :],
                    wbig_ref, hb_ref, w2_ref).astype(BF)

    ch = 512

    def cast_body(c, carry):
        r0 = pl.multiple_of(c * ch, ch)
        kvs_ref[pl.ds(r0, ch), :] = sel_ref[slot, pl.ds(r0, ch), :].astype(BF)
        return carry

    lax.fori_loop(0, past // ch, cast_body, 0)
    zpad = jnp.zeros((NEW_PAD - ds, KV_LANES), F32)
    kvs_ref[past:past + NEW_PAD, :] = jnp.concatenate([kvpn_ref[0][:, KV_LANES:], zpad], axis=0).astype(BF)
    kww_ref[0:wb, :] = win_ref[0].astype(BF)
    kww_ref[wb:wb + NEW_PAD, :] = jnp.concatenate([kvwn_ref[0], zpad], axis=0).astype(BF)

    qs = _stack_q(q_ref[0], ds)
    t, slope = _row_info(ds, g, past)
    ns = -(-(past + ds) // SEL_BLOCK)
    o_c, sel = _cmp_select(qs, kvc, t, slope, ds, n - 1, ns, m_ref)

    kvs = kvs_ref[...]
    ex = _mm(jnp.concatenate([sel.astype(BF)] * HPG, axis=0), e_ref[...])
    dist = t - lax.broadcasted_iota(jnp.int32, (1, past + NEW_PAD), 1)
    p = _msoftmax(_mm_t(qs, kvs) - slope * dist.astype(F32), (ex > 0.5) & (dist >= 0))
    o_s = _mm(p.astype(BF), kvs)

    kww = kww_ref[...]
    kpos = (past - wb) + lax.broadcasted_iota(jnp.int32, (1, wb + NEW_PAD), 1)
    dist = t - kpos
    p = _msoftmax(_mm_t(qs, kww) - slope * dist.astype(F32), (dist >= 0) & (dist <= WINDOW) & (kpos >= 0))
    o_w = _mm(p.astype(BF), kww)

    o_ref[0] = _combine(gt_ref[0, 0], o_c, o_s, o_w, ds).astype(o_ref.dtype)


def _nsa_sample_call(page_table, q, kvp, kvw, win, gates, cache, wbig, hb, w2):
    db, ds, _ = q.shape
    past = page_table.shape[1] * PAGE_SIZE
    wb = win.shape[1]
    assert ds < CMP_STRIDE and ds & (ds - 1) == 0 and ds % 8 == 0 and past % 512 == 0
    nc = past // CMP_STRIDE
    ns = -(-(past + ds) // SEL_BLOCK)
    nbp = -(-ns // LANES) * LANES
    m_mat, e_mat = _sel_matrices(nc, nc - 1, nbp, ns, np.arange(past + NEW_PAD) // SEL_BLOCK)
    grid_spec = pltpu.PrefetchScalarGridSpec(
        num_scalar_prefetch=1,
        grid=(db, KV_HEADS),
        in_specs=[pl.BlockSpec((1, ds, Q_LANES), lambda b, g, pt: (b, 0, g)),
                  pl.BlockSpec((1, ds, 2 * KV_LANES), lambda b, g, pt: (b, 0, g)),
                  pl.BlockSpec((1, ds, KV_LANES), lambda b, g, pt: (b, 0, g)),
                  pl.BlockSpec((1, wb, KV_LANES), lambda b, g, pt: (b, 0, g)),
                  pl.BlockSpec((1, 1, ds, N_GATE), lambda b, g, pt: (g, b, 0, 0)),
                  pl.BlockSpec(memory_space=pl.ANY),
                  _const_spec(wbig.shape), _const_spec(hb.shape), _const_spec(w2.shape),
                  _const_spec(e_mat.shape), _const_spec(m_mat.shape)],
        out_specs=pl.BlockSpec((1, ds, Q_LANES), lambda b, g, pt: (b, 0, g)),
        scratch_shapes=[pltpu.VMEM((2, past, KV_LANES), F32),
                        pltpu.VMEM((2, past, KV_LANES), F32),
                        pltpu.VMEM((past + NEW_PAD, KV_LANES), BF),
                        pltpu.VMEM((wb + NEW_PAD, KV_LANES), BF),
                        pltpu.SemaphoreType.DMA((2,))])
    return pl.pallas_call(
        functools.partial(_nsa_sample_kernel, past=past, ds=ds, wb=wb),
        grid_spec=grid_spec,
        out_shape=jax.ShapeDtypeStruct(q.shape, BF),
        compiler_params=_cparams(("arbitrary", "arbitrary")),
    )(page_table, q, kvp, kvw, win, gates, cache, wbig, hb, w2, e_mat, m_mat)


def _layer(x, mod, attn_fn, conv_hist, w, tiles):
    (tb_f, rb_f), (tb_p, rb_p), (tb_c, rb_c, rs_c) = tiles
    h1 = _ffn_call(x, mod, 0, w['g1'], w['ff1_in'], w['ff1_out'], tb_f, rb_f)[0]
    q, kvp, kvpb, kvw, kvwb, gates, u, sga, sgc = _proj_call(h1, mod, w['g2'], w['proj'], tb_p, rb_p)
    conv = _conv_call(u, conv_hist, w['w_dw'], w['b_dw'], w['g_cln'], w['b_cln'], tb_c, rb_c, rs_c)
    attn = attn_fn(q, kvp, kvpb, kvw, kvwb, gates)
    h2 = _merge_call(h1, attn, conv, sga, sgc, mod, w['br_attn'], w['br_conv'], w['out'], tb_f, rb_f)
    _, y = _ffn_call(h2, mod, 6, w['g3'], w['ff2_in'], w['ff2_out'], tb_f, rb_f, gf=w['g_final'])
    return y, kvp, kvw, u


def kernel(x_prompt, x_sample, c_prompt, c_sample, cache_kv, state_win, state_conv, page_table, w_ada, b_ada,
           g_norm1, g_norm2, g_norm3, g_final, w_ff1_in, w_ff1_out, w_ff2_in, w_ff2_out, w_in, pe_cmp, w_phi1,
           b_phi1, w_phi2, w_dw, b_dw, g_conv_ln, b_conv_ln, w_br_attn, w_br_conv, w_out):
    assert w_ada.shape[0] == 1, "single layer"
    b, t_len, d = x_prompt.shape
    db, ds, _ = x_sample.shape
    wb = state_win.shape[1]
    cw1 = CONV_WIDTH - 1
    q_w = N_HEADS * HEAD_DIM
    kvp_w = KV_HEADS * 4 * HEAD_DIM
    kvw_w = KV_HEADS * 2 * HEAD_DIM
    ng_w = 3 * N_HEADS
    c_dim = w_dw.shape[2]
    assert t_len >= wb and t_len >= cw1

    wi = w_in[0]
    o0, o1, o2, o3, o4 = np.cumsum([q_w, kvp_w, kvw_w, ng_w, 2 * c_dim])
    perm = np.array([r * N_HEADS + g * HPG + h for g in range(KV_HEADS) for r in range(3) for h in range(HPG)])
    w_ng = jnp.pad(wi[:, o2:o3][:, perm], ((0, 0), (0, LANES - ng_w)))
    proj = tuple(v.astype(BF) for v in (wi[:, :o0] * HEAD_DIM ** -0.5, wi[:, o0:o1], wi[:, o1:o2], w_ng,
                                        wi[:, o3:o4], wi[:, o4:]))
    w = dict(g1=g_norm1[0], g2=g_norm2[0], g3=g_norm3[0], g_final=g_final,
             ff1_in=w_ff1_in[0].astype(BF), ff1_out=w_ff1_out[0].astype(BF),
             ff2_in=w_ff2_in[0].astype(BF), ff2_out=w_ff2_out[0].astype(BF), proj=proj,
             w_dw=w_dw[0], b_dw=b_dw[0], g_cln=g_conv_ln[0], b_cln=b_conv_ln[0],
             br_attn=w_br_attn[0].astype(BF), br_conv=w_br_conv[0].astype(BF), out=w_out[0].astype(BF))
    wbig, w2 = _cmp_weights(w_phi1[0], w_phi2[0])
    hb = _cmp_bias_call(pe_cmp[0], wbig, b_phi1[0])

    nseq = b + db
    pad = -nseq % 8
    c_all = jnp.concatenate([c_prompt, c_sample, jnp.zeros((pad, d), F32)], axis=0)
    mod = _mod_call(c_all, w_ada[0], b_ada[0])
    mod_p = mod[:b].reshape(b, 1, N_MOD * d)
    mod_s = mod[b:nseq].reshape(db, 1, N_MOD * d)

    attn_p = lambda q, kvp, kvpb, kvw, kvwb, gates: _nsa_prompt_call(q, kvp, kvpb, kvwb, gates, wbig, hb, w2)
    y_p, kvp_p, kvw_p, u_p = _layer(x_prompt, mod_p, attn_p, None, w,
                                    ((1, 512), (1, 256), (1, 256, 32)))

    cache = cache_kv.reshape(cache_kv.shape[0], PAGE_SIZE, -1)
    assert cache_kv.shape[2] == 1
    win = state_win[:, :, 0].reshape(db, wb, kvw_w)
    attn_s = lambda q, kvp, kvpb, kvw, kvwb, gates: _nsa_sample_call(page_table, q, kvp, kvw, win, gates,
                                                                      cache, wbig, hb, w2)
    sb = min(db, 64)
    y_s, kvp_s, kvw_s, u_s = _layer(x_sample, mod_s, attn_s, state_conv[:, :, 0], w,
                                    ((sb, ds), (min(db, 32), ds), (min(db, 16), ds, ds)))

    kv5 = lambda a: a.reshape(a.shape[0], a.shape[1], 1, KV_HEADS, 4, HEAD_DIM)
    win5 = lambda a: a.reshape(a.shape[0], a.shape[1], 1, KV_HEADS, 2, HEAD_DIM)
    new_win_p = win5(kvw_p[:, t_len - wb:])
    new_win_s = win5(jnp.concatenate([win, kvw_s], axis=1)[:, -wb:])
    new_conv_p = u_p[:, t_len - cw1:][:, :, None, :]
    new_conv_s = jnp.concatenate([state_conv[:, :, 0], u_s], axis=1)[:, -cw1:][:, :, None, :]
    return (y_p, y_s, kv5(kvp_p), kv5(kvp_s), new_win_p, new_win_s, new_conv_p, new_conv_s)
```

```python
import functools

import numpy as np
import jax
import jax.numpy as jnp
from jax import lax
from jax.experimental import pallas as pl
from jax.experimental.pallas import tpu as pltpu

F32 = jnp.float32
BF = jnp.bfloat16

N_HEADS = 16
HEAD_DIM = 64
KV_HEADS = 4
HPG = N_HEADS // KV_HEADS
CMP_LEN = 32
CMP_STRIDE = 16
SEL_BLOCK = 64
N_SEL = 16
WINDOW = 512
PHI_HID = 128
CONV_WIDTH = 31
N_MOD = 9
EPS = 1e-6
NEG = -1e30
FORCE = 1e4
PAGE_SIZE = 128

LANES = 128
KV_LANES = 2 * HEAD_DIM
Q_LANES = HPG * HEAD_DIM
N_GATE = 3 * HPG
N_POS = 6
VMEM_LIMIT = 56 * 1024 * 1024


def _cparams(sem):
    return pltpu.CompilerParams(dimension_semantics=sem, vmem_limit_bytes=VMEM_LIMIT)


def _const_spec(shape):
    nd = len(shape)
    return pl.BlockSpec(shape, lambda *_: (0,) * nd, pipeline_mode=pl.Buffered(1))


def _tok(ref):
    tb, rb, w = ref.shape
    return ref[0] if tb == 1 else ref[...].reshape(tb * rb, w)


def _modrow(ref, rb):
    tb, _, w = ref.shape
    if tb == 1:
        return ref[0]
    return jnp.broadcast_to(ref[...], (tb, rb, w)).reshape(tb * rb, w)


def _put(ref, val):
    tb, rb, w = ref.shape
    if tb == 1:
        ref[0] = val.astype(ref.dtype)
    else:
        ref[...] = val.reshape(tb, rb, w).astype(ref.dtype)


def _rms(x, g):
    return x * lax.rsqrt(jnp.mean(x * x, axis=-1, keepdims=True) + EPS) * g


def _silu(x):
    return x * jax.nn.sigmoid(x)


def _mm(a, b):
    return jnp.dot(a, b, preferred_element_type=F32)


def _mm_t(a, b):
    return lax.dot_general(a, b, (((1,), (1,)), ((), ())), preferred_element_type=F32)


def _tok_spec(tb, rb, w):
    return pl.BlockSpec((tb, rb, w), lambda i, j: (i, j, 0))


def _mod_spec(tb, d, col):
    return pl.BlockSpec((tb, 1, d), lambda i, j: (i, 0, col))


def _mod_kernel(c_ref, w_ref, b_ref, o_ref):
    a = _silu(c_ref[...]).astype(BF)
    o_ref[...] = _mm(a, w_ref[...].astype(BF)) + b_ref[...]


def _mod_call(c, w, b):
    m, d = c.shape
    n = w.shape[1]
    tn = d
    return pl.pallas_call(
        _mod_kernel,
        grid=(n // tn,),
        in_specs=[pl.BlockSpec((m, d), lambda j: (0, 0)),
                  pl.BlockSpec((d, tn), lambda j: (0, j)),
                  pl.BlockSpec((1, tn), lambda j: (0, j))],
        out_specs=pl.BlockSpec((m, tn), lambda j: (0, j)),
        out_shape=jax.ShapeDtypeStruct((m, n), F32),
        compiler_params=_cparams(("arbitrary",)),
    )(c, w, b.reshape(1, n))


def _ffn_kernel(*refs, d_ff, fc, final):
    if final:
        x_ref, sh_ref, sc_ref, gt_ref, g_ref, wi_ref, wo_ref, gf_ref, h_ref, y_ref = refs
    else:
        x_ref, sh_ref, sc_ref, gt_ref, g_ref, wi_ref, wo_ref, h_ref = refs
    rb = x_ref.shape[1]
    x = _tok(x_ref)
    n = (_rms(x, g_ref[...]) * (1.0 + _modrow(sc_ref, rb)) + _modrow(sh_ref, rb)).astype(BF)
    acc = jnp.zeros(x.shape, F32)
    for c in range(d_ff // fc):
        gate = _mm(n, wi_ref[:, c * fc:(c + 1) * fc])
        up = _mm(n, wi_ref[:, d_ff + c * fc:d_ff + (c + 1) * fc])
        acc = acc + _mm((_silu(gate) * up).astype(BF), wo_ref[c * fc:(c + 1) * fc, :])
    h = x + _modrow(gt_ref, rb) * (0.5 * acc)
    _put(h_ref, h)
    if final:
        _put(y_ref, _rms(h, gf_ref[...]))


def _ffn_call(x, mod, col0, g, wi, wo, tb, rb, gf=None):
    nb, r, d = x.shape
    d_ff = wo.shape[0]
    fc = d_ff // 2
    final = gf is not None
    ins = [x, mod, mod, mod, g.reshape(1, d), wi, wo]
    specs = [_tok_spec(tb, rb, d), _mod_spec(tb, d, col0), _mod_spec(tb, d, col0 + 1),
             _mod_spec(tb, d, col0 + 2), _const_spec((1, d)), _const_spec(wi.shape), _const_spec(wo.shape)]
    out_shape = [jax.ShapeDtypeStruct(x.shape, F32)]
    out_specs = [_tok_spec(tb, rb, d)]
    if final:
        ins.append(gf.reshape(1, d))
        specs.append(_const_spec((1, d)))
        out_shape.append(jax.ShapeDtypeStruct(x.shape, F32))
        out_specs.append(_tok_spec(tb, rb, d))
    return pl.pallas_call(
        functools.partial(_ffn_kernel, d_ff=d_ff, fc=fc, final=final),
        grid=(nb // tb, r // rb),
        in_specs=specs, out_specs=out_specs, out_shape=out_shape,
        compiler_params=_cparams(("arbitrary", "arbitrary")),
    )(*ins)


def _proj_kernel(h_ref, sh_ref, sc_ref, g_ref, wq_ref, wkvp_ref, wkvw_ref, wng_ref, wglu_ref, wmg_ref,
                 q_ref, kvp_ref, kvpb_ref, kvw_ref, kvwb_ref, gates_ref, u_ref, sga_ref, sgc_ref):
    tb, rb, d = h_ref.shape
    n = (_rms(_tok(h_ref), g_ref[...]) * (1.0 + _modrow(sc_ref, rb)) + _modrow(sh_ref, rb)).astype(BF)
    _put(q_ref, _mm(n, wq_ref[...]))
    kvp = _mm(n, wkvp_ref[...])
    _put(kvp_ref, kvp)
    _put(kvpb_ref, kvp)
    kvw = _mm(n, wkvw_ref[...])
    _put(kvw_ref, kvw)
    _put(kvwb_ref, kvw)
    ng = jax.nn.sigmoid(_mm(n, wng_ref[...]))
    for g in range(KV_HEADS):
        gates_ref[g] = ng[:, g * N_GATE:(g + 1) * N_GATE].reshape(tb, rb, N_GATE)
    glu = _mm(n, wglu_ref[...])
    c = glu.shape[1] // 2
    _put(u_ref, glu[:, :c] * jax.nn.sigmoid(glu[:, c:]))
    mg = jax.nn.sigmoid(_mm(n, wmg_ref[...]))
    _put(sga_ref, mg[:, :d])
    _put(sgc_ref, mg[:, d:])


def _proj_call(h, mod, g, ws, tb, rb):
    nb, r, d = h.shape
    wq, wkvp, wkvw, wng, wglu, wmg = ws
    c = wglu.shape[1] // 2
    widths = [(wq.shape[1], BF), (wkvp.shape[1], F32), (wkvp.shape[1], BF), (wkvw.shape[1], F32),
              (wkvw.shape[1], BF)]
    out_shape = [jax.ShapeDtypeStruct((nb, r, w), dt) for w, dt in widths]
    out_specs = [_tok_spec(tb, rb, w) for w, _ in widths]
    out_shape.append(jax.ShapeDtypeStruct((KV_HEADS, nb, r, N_GATE), F32))
    out_specs.append(pl.BlockSpec((KV_HEADS, tb, rb, N_GATE), lambda i, j: (0, i, j, 0)))
    for w in (c, d, d):
        out_shape.append(jax.ShapeDtypeStruct((nb, r, w), F32))
        out_specs.append(_tok_spec(tb, rb, w))
    return pl.pallas_call(
        _proj_kernel,
        grid=(nb // tb, r // rb),
        in_specs=[_tok_spec(tb, rb, d), _mod_spec(tb, d, 3), _mod_spec(tb, d, 4), _const_spec((1, d))]
                 + [_const_spec(w.shape) for w in ws],
        out_specs=out_specs, out_shape=out_shape,
        compiler_params=_cparams(("arbitrary", "arbitrary")),
    )(h, mod, mod, g.reshape(1, d), *ws)


CONV_HIST = 32


def _conv_kernel(u_ref, hist_ref, w_ref, b_ref, g_ref, bl_ref, o_ref, ext_ref, *, zero_first, rs):
    tb, rb, c = u_ref.shape
    hr = hist_ref.shape[1]
    hist = hist_ref[...]
    if zero_first:
        hist = jnp.where(pl.program_id(1) > 0, hist, 0.0)
    ext_ref[:, CONV_HIST - hr:CONV_HIST, :] = hist
    ext_ref[:, CONV_HIST:CONV_HIST + rb, :] = u_ref[...]
    off = CONV_HIST - (CONV_WIDTH - 1)
    w = w_ref[...]
    for s in range(tb):
        for r0 in range(0, rb, rs):
            acc = jnp.broadcast_to(b_ref[...], (rs, c))
            for k in range(CONV_WIDTH):
                acc = acc + w[k:k + 1, :] * ext_ref[s, r0 + k + off:r0 + k + off + rs, :]
            xc = acc - jnp.mean(acc, axis=-1, keepdims=True)
            y = xc * lax.rsqrt(jnp.mean(xc * xc, axis=-1, keepdims=True) + EPS) * g_ref[...] + bl_ref[...]
            o_ref[s, r0:r0 + rs, :] = _silu(y).astype(o_ref.dtype)


def _conv_call(u, hist, w, b, g, bl, tb, rb, rs):
    nb, r, c = u.shape
    row = lambda v: v.reshape(1, c)
    if hist is None:
        hist_arr, hr = u, CONV_HIST
        hspec = pl.BlockSpec((tb, hr, c), lambda i, j: (i, jnp.maximum(j * (rb // CONV_HIST) - 1, 0), 0))
    else:
        hist_arr, hr = hist, hist.shape[1]
        hspec = pl.BlockSpec((tb, hr, c), lambda i, j: (i, 0, 0))
    return pl.pallas_call(
        functools.partial(_conv_kernel, zero_first=hist is None, rs=rs),
        grid=(nb // tb, r // rb),
        in_specs=[_tok_spec(tb, rb, c), hspec, _const_spec(w.shape)] + [_const_spec((1, c))] * 3,
        out_specs=_tok_spec(tb, rb, c),
        out_shape=jax.ShapeDtypeStruct(u.shape, BF),
        scratch_shapes=[pltpu.VMEM((tb, CONV_HIST + rb, c), F32)],
        compiler_params=_cparams(("arbitrary", "arbitrary")),
    )(u, hist_arr, w, row(b), row(g), row(bl))


def _merge_kernel(h_ref, attn_ref, conv_ref, sga_ref, sgc_ref, gt_ref, wa_ref, wc_ref, wo_ref, o_ref):
    rb = h_ref.shape[1]
    m = _tok(sga_ref) * _mm(_tok(attn_ref), wa_ref[...]) + _tok(sgc_ref) * _mm(_tok(conv_ref), wc_ref[...])
    _put(o_ref, _tok(h_ref) + _modrow(gt_ref, rb) * _mm(m.astype(BF), wo_ref[...]))


def _merge_call(h, attn, conv, sga, sgc, mod, wa, wc, wo, tb, rb):
    nb, r, d = h.shape
    return pl.pallas_call(
        _merge_kernel,
        grid=(nb // tb, r // rb),
        in_specs=[_tok_spec(tb, rb, d), _tok_spec(tb, rb, attn.shape[2]), _tok_spec(tb, rb, conv.shape[2]),
                  _tok_spec(tb, rb, d), _tok_spec(tb, rb, d), _mod_spec(tb, d, 5),
                  _const_spec(wa.shape), _const_spec(wc.shape), _const_spec(wo.shape)],
        out_specs=_tok_spec(tb, rb, d),
        out_shape=jax.ShapeDtypeStruct(h.shape, F32),
        compiler_params=_cparams(("arbitrary", "arbitrary")),
    )(h, attn, conv, sga, sgc, mod, wa, wc, wo)


def _cmp_weights(w_phi1, w_phi2):
    c = CMP_LEN // CMP_STRIDE
    w = w_phi1.reshape(2, c, CMP_STRIDE, HEAD_DIM, PHI_HID)
    eye = jnp.eye(2, dtype=w.dtype)
    wbig = jnp.einsum('kosdh,kc->skdcoh', w, eye).reshape(CMP_STRIDE * KV_LANES, 2 * c * PHI_HID)
    w2 = jnp.einsum('khd,kc->khcd', w_phi2, eye).reshape(2 * PHI_HID, KV_LANES)
    return wbig.astype(BF), w2.astype(BF)


def _cmp_bias_kernel(pe_ref, wbig_ref, b_ref, o_ref):
    res = _mm(pe_ref[...].astype(BF), wbig_ref[...])
    hs = [b_ref[:, k * PHI_HID:(k + 1) * PHI_HID]
          + res[0:1, 2 * k * PHI_HID:(2 * k + 1) * PHI_HID]
          + res[1:2, (2 * k + 1) * PHI_HID:(2 * k + 2) * PHI_HID] for k in range(2)]
    o_ref[...] = jnp.concatenate(hs, axis=1)


def _cmp_bias_call(pe_cmp, wbig, b_phi1):
    c = CMP_LEN // CMP_STRIDE
    pe = pe_cmp.reshape(2, c, CMP_STRIDE, HEAD_DIM).transpose(1, 2, 0, 3).reshape(c, CMP_STRIDE * KV_LANES)
    pe = jnp.concatenate([pe, jnp.zeros((8 - c, pe.shape[1]), pe.dtype)], axis=0)
    return pl.pallas_call(
        _cmp_bias_kernel,
        out_shape=jax.ShapeDtypeStruct((1, 2 * PHI_HID), F32),
        compiler_params=pltpu.CompilerParams(vmem_limit_bytes=VMEM_LIMIT),
    )(pe, wbig, b_phi1.reshape(1, 2 * PHI_HID))


def _compress(load_rows, wbig_ref, hb_ref, w2_ref):
    lhs = jnp.concatenate([load_rows(s).astype(BF) for s in range(CMP_STRIDE)], axis=1)
    part = _mm(lhs, wbig_ref[...])
    n = part.shape[0]
    nxt = pltpu.roll(part, n - 1, 0)
    hb = hb_ref[...]
    hs = [hb[:, k * PHI_HID:(k + 1) * PHI_HID]
          + part[:, 2 * k * PHI_HID:(2 * k + 1) * PHI_HID]
          + nxt[:, (2 * k + 1) * PHI_HID:(2 * k + 2) * PHI_HID] for k in range(2)]
    h = jnp.concatenate(hs, axis=1)
    return _mm(_silu(h).astype(BF), w2_ref[...])


def _stack_q(q, r):
    q = q.astype(F32)
    return jnp.concatenate([q[:, h * HEAD_DIM:(h + 1) * HEAD_DIM] for h in range(HPG)], axis=0)


def _row_info(r, g, t0):
    row = lax.broadcasted_iota(jnp.int32, (HPG * r, 1), 0)
    t = t0 + (row & (r - 1))
    head = g * HPG + lax.shift_right_logical(row, r.bit_length() - 1)
    slope = jnp.exp2(-(8.0 / N_HEADS) * (head + 1).astype(F32))
    return t, slope


def _q_aux(slope, width, pos0, selneg=None):
    hi = slope.astype(BF).astype(F32)
    rest = slope - hi
    mid = rest.astype(BF).astype(F32)
    lo = rest - mid
    lane = lax.broadcasted_iota(jnp.int32, (slope.shape[0], width), 1) - pos0
    base = jnp.zeros((slope.shape[0], width), F32) if selneg is None else selneg
    aux = jnp.where((lane == 0) | (lane == 3), hi, base)
    aux = jnp.where((lane == 1) | (lane == 4), mid, aux)
    aux = jnp.where((lane == 2) | (lane == 5), lo, aux)
    return aux.astype(BF)


def _k_aux(pos, blocks, width, pos0):
    pos = np.asarray(pos)
    a = np.zeros((pos.shape[0], width), np.float32)
    if blocks is not None:
        a[np.arange(pos.shape[0]), np.asarray(blocks)] = 1.0
    a[:, pos0:pos0 + 3] = (pos // SEL_BLOCK * SEL_BLOCK)[:, None]
    a[:, pos0 + 3:pos0 + 6] = (pos % SEL_BLOCK)[:, None]
    return a


def _topk_mask(xt, k):
    nb = xt.shape[0]
    rowf = lax.broadcasted_iota(jnp.int32, xt.shape, 0).astype(F32)

    def body(_, carry):
        x, sel = carry
        m = jnp.max(x, axis=0, keepdims=True)
        idx = jnp.min(jnp.where(x == m, rowf, float(nb)), axis=0, keepdims=True)
        pick = rowf == idx
        return jnp.where(pick, -3.0, x), jnp.where(pick, 1.0, sel)

    _, sel = lax.fori_loop(0, k, body, (xt, jnp.zeros_like(xt)))
    return sel


def _select(score, ns):
    nbp = score.shape[1]
    nb8 = -(-ns // 8) * 8
    sel_t = _topk_mask(score.T[0:nb8], N_SEL)
    if nb8 < nbp:
        sel_t = jnp.concatenate([sel_t, jnp.zeros((nbp - nb8, sel_t.shape[1]), F32)], axis=0)
    return sel_t.T


def _cmp_branch(qx, kvc, caux_ref, t, r, n_cmp, ns, m_ref):
    nc = kvc.shape[0]
    nbp = m_ref.shape[1]
    ci = lax.broadcasted_iota(jnp.int32, (1, nc), 1)
    mask = (t >= ci * CMP_STRIDE + (CMP_LEN - 1)) & (ci < n_cmp)
    s = jnp.where(mask, _mm_t(qx, jnp.concatenate([kvc, caux_ref[...]], axis=1)), NEG)
    e = jnp.where(mask, jnp.exp(s - jnp.max(s, axis=-1, keepdims=True)), 0.0)
    p = e / jnp.maximum(jnp.sum(e, axis=-1, keepdims=True), 1e-30)
    o_c = _mm(p.astype(BF), kvc)
    psum = p[0:r] + p[r:2 * r] + p[2 * r:3 * r] + p[3 * r:4 * r]
    imp = jnp.dot(psum, m_ref[...], precision=lax.Precision.HIGHEST, preferred_element_type=F32)
    blk = lax.broadcasted_iota(jnp.int32, (1, nbp), 1)
    cur = lax.shift_right_logical(t[0:r], SEL_BLOCK.bit_length() - 1)
    forced = (blk == 0) | (blk == cur) | (blk == cur - 1)
    score = jnp.where(blk <= cur, imp + jnp.where(forced, FORCE, 0.0), -1.0)
    return o_c, jnp.where(blk < ns, score, -2.0)


def _gate_col(gt, branch):
    return jnp.concatenate([gt[:, branch * HPG + h:branch * HPG + h + 1] for h in range(HPG)], axis=0)


def _unstack(o, r):
    return jnp.concatenate([o[h * r:(h + 1) * r] for h in range(HPG)], axis=1)


def _imp_matrix(nc, n_cmp, nbp, ns):
    i = np.arange(nc)[:, None]
    j = np.arange(nbp)[None, :]
    r = SEL_BLOCK // CMP_STRIDE
    c = CMP_LEN // CMP_STRIDE
    return jnp.asarray((i >= r * j - (c - 1)) & (i <= r * j + r - 1) & (i < n_cmp) & (j < ns), F32)


def _nsa_prompt_kernel(q_ref, kc_ref, ks_ref, kw_ref, gt_ref, wbig_ref, hb_ref, w2_ref, kaux_ref, caux_ref,
                       m_ref, o_ref, kvc_ref, *, tq, t_len, ck, ns):
    g = pl.program_id(1)
    i = pl.program_id(2)
    n = t_len // CMP_STRIDE
    rows = HPG * tq

    @pl.when(i == 0)
    def _():
        kvc = _compress(lambda s: kc_ref[0, pl.ds(s, n, stride=CMP_STRIDE), :], wbig_ref, hb_ref, w2_ref)
        kvc_ref[...] = kvc.astype(BF)

    start = i * tq
    t, slope = _row_info(tq, g, start)
    qs = _stack_q(q_ref[0], tq)
    qk = jnp.concatenate([qs, jnp.zeros_like(qs)], axis=1).astype(BF)
    qx = jnp.concatenate([qk, _q_aux(slope, LANES, ns)], axis=1)

    o_c, score = _cmp_branch(qx, kvc_ref[...], caux_ref, t, tq, n - 1, ns, m_ref)
    sel = _select(score, ns)

    lo = pl.multiple_of(jnp.maximum(start - WINDOW, 0), tq)
    kw = kw_ref[0, pl.ds(lo, WINDOW + tq), :]
    dist = t - (lo + lax.broadcasted_iota(jnp.int32, (1, WINDOW + tq), 1))
    s = _mm_t(qx, jnp.concatenate([kw, kaux_ref[pl.ds(lo, WINDOW + tq), :]], axis=1))
    s = jnp.where((dist >= 0) & (dist <= WINDOW), s, NEG)
    e = jnp.exp(s - jnp.max(s, axis=-1, keepdims=True))
    o_w = _mm(e.astype(BF), kw) / jnp.sum(e, axis=-1, keepdims=True)

    selneg = jnp.concatenate([(sel - 1.0) * (-NEG)] * HPG, axis=0)
    qx = jnp.concatenate([qk, _q_aux(slope, LANES, ns, selneg)], axis=1)

    def chunk(c, carry, causal):
        m, l, acc = carry
        k0 = pl.multiple_of(c * ck, ck)
        kv = ks_ref[0, pl.ds(k0, ck), :]
        s = _mm_t(qx, jnp.concatenate([kv, kaux_ref[pl.ds(k0, ck), :]], axis=1))
        if causal:
            s = jnp.where(t >= k0 + lax.broadcasted_iota(jnp.int32, (1, ck), 1), s, NEG)
        m_new = jnp.maximum(m, jnp.max(s, axis=-1, keepdims=True))
        alpha = jnp.exp(m - m_new)
        e = jnp.exp(s - m_new)
        l = alpha * l + jnp.sum(e, axis=-1, keepdims=True)
        acc = alpha * acc + _mm(e.astype(BF), kv)
        return m_new, l, acc

    init = (jnp.full((rows, 1), NEG, F32), jnp.zeros((rows, 1), F32), jnp.zeros((rows, KV_LANES), F32))
    last = (start + tq - 1) // ck
    carry = lax.fori_loop(0, last, lambda c, cr: chunk(c, cr, False), init)
    _, l, acc = chunk(last, carry, True)
    o_s = acc / l

    gt = gt_ref[0, 0]
    o = _gate_col(gt, 0) * o_c + _gate_col(gt, 1) * o_s + _gate_col(gt, 2) * o_w
    o_ref[0] = _unstack(o[:, HEAD_DIM:], tq).astype(o_ref.dtype)


def _nsa_prompt_call(q, kvp, kvpb, kvwb, gates, wbig, hb, w2, tq=128, ck=512):
    b, t_len, _ = q.shape
    nc = t_len // CMP_STRIDE
    ns = t_len // SEL_BLOCK
    assert t_len % ck == 0 and t_len >= WINDOW + tq and N_SEL <= ns <= LANES - N_POS and tq == LANES
    pos = np.arange(t_len)
    kaux = jnp.asarray(_k_aux(pos, pos // SEL_BLOCK, LANES, ns), BF)
    caux = jnp.asarray(_k_aux(np.arange(nc) * CMP_STRIDE + CMP_LEN - 1, None, LANES, ns), BF)
    m_mat = _imp_matrix(nc, nc - 1, LANES, ns)
    return pl.pallas_call(
        functools.partial(_nsa_prompt_kernel, tq=tq, t_len=t_len, ck=ck, ns=ns),
        grid=(b, KV_HEADS, t_len // tq),
        in_specs=[pl.BlockSpec((1, tq, Q_LANES), lambda b, g, i: (b, i, g)),
                  pl.BlockSpec((1, t_len, KV_LANES), lambda b, g, i: (b, 0, 2 * g)),
                  pl.BlockSpec((1, t_len, KV_LANES), lambda b, g, i: (b, 0, 2 * g + 1)),
                  pl.BlockSpec((1, t_len, KV_LANES), lambda b, g, i: (b, 0, g)),
                  pl.BlockSpec((1, 1, tq, N_GATE), lambda b, g, i: (g, b, i, 0)),
                  _const_spec(wbig.shape), _const_spec(hb.shape), _const_spec(w2.shape),
                  _const_spec(kaux.shape), _const_spec(caux.shape), _const_spec(m_mat.shape)],
        out_specs=pl.BlockSpec((1, tq, Q_LANES), lambda b, g, i: (b, i, g)),
        out_shape=jax.ShapeDtypeStruct(q.shape, BF),
        scratch_shapes=[pltpu.VMEM((nc, KV_LANES), BF)],
        compiler_params=_cparams(("arbitrary", "arbitrary", "arbitrary")),
    )(q, kvp, kvpb, kvwb, gates, wbig, hb, w2, kaux, caux, m_mat)


NEW_PAD = 128


def _page_stream(pt_ref, cache_ref, sem, row0, n_pages, dst_of):
    b = pl.program_id(0)
    g = pl.program_id(1)
    step = b * KV_HEADS + g
    total = pl.num_programs(0) * KV_HEADS
    slot = step % 2

    def copy(bb, gg, p, sl):
        rows = pl.ds(pl.multiple_of(gg * 2 * KV_LANES + row0, KV_LANES), KV_LANES)
        return pltpu.make_async_copy(cache_ref.at[pt_ref[bb, p], rows, :], dst_of(sl, p), sem.at[sl])

    def fetch(st, sl):
        bb = st // KV_HEADS
        gg = st % KV_HEADS

        def body(p, carry):
            copy(bb, gg, p, sl).start()
            return carry

        lax.fori_loop(0, n_pages, body, 0, unroll=4)

    @pl.when(step == 0)
    def _():
        fetch(step, slot)

    @pl.when(step + 1 < total)
    def _():
        fetch(step + 1, 1 - slot)

    def wait_body(p, carry):
        copy(b, g, p, slot).wait()
        return carry

    lax.fori_loop(0, n_pages, wait_body, 0, unroll=4)
    return slot


def _sample_cmp_kernel(pt_ref, q_ref, cache_ref, wbig_ref, hb_ref, w2_ref, caux_ref, m_ref,
                       oc_ref, score_ref, buf_ref, nat_ref, sem, *, past, ds, ns):
    g = pl.program_id(1)
    n_pages = past // PAGE_SIZE
    slot = _page_stream(pt_ref, cache_ref, sem, 0, n_pages, lambda sl, p: buf_ref.at[sl, p])

    def xpose(p, carry):
        nat_ref[pl.ds(pl.multiple_of(p * PAGE_SIZE, PAGE_SIZE), PAGE_SIZE), :] = buf_ref[slot, p].T
        return carry

    lax.fori_loop(0, n_pages, xpose, 0, unroll=min(16, n_pages))
    n = past // CMP_STRIDE
    kvc = _compress(lambda s: nat_ref[pl.ds(s, n, stride=CMP_STRIDE), :], wbig_ref, hb_ref, w2_ref).astype(BF)

    t, slope = _row_info(ds, g, past)
    qs = _stack_q(q_ref[0], ds)
    qx = jnp.concatenate([qs, jnp.zeros_like(qs)], axis=1).astype(BF)
    qx = jnp.concatenate([qx, _q_aux(slope, LANES, 0)], axis=1)
    o_c, score = _cmp_branch(qx, kvc, caux_ref, t, ds, n - 1, ns, m_ref)
    oc_ref[0, 0] = o_c
    score_ref[0, 0] = score


def _topk_kernel(score_ref, sel_ref, *, ns):
    sel_ref[...] = _select(score_ref[...], ns)


def _sample_sel_kernel(pt_ref, q_ref, kvpn_ref, kvwn_ref, win_ref, gt_ref, oc_ref, sel_ref, cache_ref, eaux_ref,
                       o_ref, buf_ref, sem, *, past, ds, wb, ns):
    g = pl.program_id(1)
    n_pages = past // PAGE_SIZE
    slot = _page_stream(pt_ref, cache_ref, sem, KV_LANES, n_pages,
                        lambda sl, p: buf_ref.at[sl, :, pl.ds(pl.multiple_of(p * PAGE_SIZE, PAGE_SIZE), PAGE_SIZE)])

    t, slope = _row_info(ds, g, past)
    qs = _stack_q(q_ref[0], ds).astype(BF)
    zpad = jnp.zeros((NEW_PAD - ds, KV_LANES), F32)
    lane = lax.broadcasted_iota(jnp.int32, (1, NEW_PAD), 1)

    def new_t(rows):
        x = jnp.concatenate([rows, zpad], axis=0).T.astype(BF)
        return x[0:HEAD_DIM], x[HEAD_DIM:KV_LANES]

    nbp = sel_ref.shape[3]
    selneg = jnp.concatenate([(sel_ref[0, 0] - 1.0) * (-NEG)] * HPG, axis=0)
    qa = _q_aux(slope, nbp, ns, selneg)
    kn, vn = new_t(kvpn_ref[0][:, KV_LANES:])
    kt = buf_ref[slot, 0:HEAD_DIM, :].astype(BF)
    vt = buf_ref[slot, HEAD_DIM:KV_LANES, :].astype(BF)
    s_p = _mm(qs, kt) + _mm(qa, eaux_ref[:, 0:past])
    s_n = _mm(qs, kn) + _mm(qa, eaux_ref[:, past:past + NEW_PAD])
    s_n = jnp.where(t >= past + lane, s_n, NEG)
    m = jnp.maximum(jnp.max(s_p, axis=-1, keepdims=True), jnp.max(s_n, axis=-1, keepdims=True))
    e_p = jnp.exp(s_p - m)
    e_n = jnp.exp(s_n - m)
    l = jnp.sum(e_p, axis=-1, keepdims=True) + jnp.sum(e_n, axis=-1, keepdims=True)
    o_s = (_mm_t(e_p.astype(BF), vt) + _mm_t(e_n.astype(BF), vn)) / l

    kn, vn = new_t(kvwn_ref[0])
    kt = win_ref[0, 0:HEAD_DIM, :].astype(BF)
    vt = win_ref[0, HEAD_DIM:KV_LANES, :].astype(BF)

    def band(s, kpos):
        dist = t - kpos
        return jnp.where((dist >= 0) & (dist <= WINDOW) & (kpos >= 0), s - slope * dist.astype(F32), NEG)

    s_p = band(_mm(qs, kt), (past - wb) + lax.broadcasted_iota(jnp.int32, (1, wb), 1))
    s_n = band(_mm(qs, kn), past + lane)
    m = jnp.maximum(jnp.max(s_p, axis=-1, keepdims=True), jnp.max(s_n, axis=-1, keepdims=True))
    e_p = jnp.exp(s_p - m)
    e_n = jnp.exp(s_n - m)
    l = jnp.sum(e_p, axis=-1, keepdims=True) + jnp.sum(e_n, axis=-1, keepdims=True)
    o_w = (_mm_t(e_p.astype(BF), vt) + _mm_t(e_n.astype(BF), vn)) / l

    gt = gt_ref[0, 0]
    o = _gate_col(gt, 0) * oc_ref[0, 0][:, HEAD_DIM:] + _gate_col(gt, 1) * o_s + _gate_col(gt, 2) * o_w
    o_ref[0] = _unstack(o, ds).astype(o_ref.dtype)


def _nsa_sample_call(page_table, q, kvp, kvw, win_t, gates, cache_t, wbig, hb, w2):
    db, ds, _ = q.shape
    n_pages = page_table.shape[1]
    past = n_pages * PAGE_SIZE
    wb = win_t.shape[2]
    assert ds < CMP_STRIDE and ds & (ds - 1) == 0 and ds % 8 == 0
    nc = past // CMP_STRIDE
    ns = -(-(past + ds) // SEL_BLOCK)
    nbp = -(-(ns + N_POS) // LANES) * LANES
    rows = HPG * ds
    caux = jnp.asarray(_k_aux(np.arange(nc) * CMP_STRIDE + CMP_LEN - 1, None, LANES, 0), BF)
    m_mat = _imp_matrix(nc, nc - 1, nbp, ns)
    pos = np.arange(past + NEW_PAD)
    eaux = jnp.asarray(_k_aux(pos, pos // SEL_BLOCK, nbp, ns).T, BF)
    qspec = pl.BlockSpec((1, ds, Q_LANES), lambda b, g, pt: (b, 0, g))
    bg_spec = lambda r, w: pl.BlockSpec((1, 1, r, w), lambda b, g, pt: (b, g, 0, 0))

    o_c, score = pl.pallas_call(
        functools.partial(_sample_cmp_kernel, past=past, ds=ds, ns=ns),
        grid_spec=pltpu.PrefetchScalarGridSpec(
            num_scalar_prefetch=1, grid=(db, KV_HEADS),
            in_specs=[qspec, pl.BlockSpec(memory_space=pl.ANY),
                      _const_spec(wbig.shape), _const_spec(hb.shape), _const_spec(w2.shape),
                      _const_spec(caux.shape), _const_spec(m_mat.shape)],
            out_specs=[bg_spec(rows, KV_LANES), bg_spec(ds, nbp)],
            scratch_shapes=[pltpu.VMEM((2, n_pages, KV_LANES, PAGE_SIZE), F32),
                            pltpu.VMEM((past, KV_LANES), F32),
                            pltpu.SemaphoreType.DMA((2,))]),
        out_shape=[jax.ShapeDtypeStruct((db, KV_HEADS, rows, KV_LANES), F32),
                   jax.ShapeDtypeStruct((db, KV_HEADS, ds, nbp), F32)],
        compiler_params=_cparams(("arbitrary", "arbitrary")),
    )(page_table, q, cache_t, wbig, hb, w2, caux, m_mat)

    nq = db * KV_HEADS * ds
    nq_pad = -(-nq // LANES) * LANES
    score2 = jnp.pad(score.reshape(nq, nbp), ((0, nq_pad - nq), (0, 0)), constant_values=-2.0)
    sel = pl.pallas_call(
        functools.partial(_topk_kernel, ns=ns),
        grid=(nq_pad // LANES,),
        in_specs=[pl.BlockSpec((LANES, nbp), lambda i: (i, 0))],
        out_specs=pl.BlockSpec((LANES, nbp), lambda i: (i, 0)),
        out_shape=jax.ShapeDtypeStruct((nq_pad, nbp), F32),
        compiler_params=_cparams(("arbitrary",)),
    )(score2)[:nq].reshape(db, KV_HEADS, ds, nbp)

    return pl.pallas_call(
        functools.partial(_sample_sel_kernel, past=past, ds=ds, wb=wb, ns=ns),
        grid_spec=pltpu.PrefetchScalarGridSpec(
            num_scalar_prefetch=1, grid=(db, KV_HEADS),
            in_specs=[qspec,
                      pl.BlockSpec((1, ds, 2 * KV_LANES), lambda b, g, pt: (b, 0, g)),
                      pl.BlockSpec((1, ds, KV_LANES), lambda b, g, pt: (b, 0, g)),
                      pl.BlockSpec((1, KV_LANES, wb), lambda b, g, pt: (b, g, 0)),
                      pl.BlockSpec((1, 1, ds, N_GATE), lambda b, g, pt: (g, b, 0, 0)),
                      bg_spec(rows, KV_LANES), bg_spec(ds, nbp),
                      pl.BlockSpec(memory_space=pl.ANY), _const_spec(eaux.shape)],
            out_specs=qspec,
            scratch_shapes=[pltpu.VMEM((2, KV_LANES, past), F32), pltpu.SemaphoreType.DMA((2,))]),
        out_shape=jax.ShapeDtypeStruct(q.shape, BF),
        compiler_params=_cparams(("arbitrary", "arbitrary")),
    )(page_table, q, kvp, kvw, win_t, gates, o_c, sel, cache_t, eaux)


def _layer(x, mod, attn_fn, conv_hist, w, tiles):
    (tb_f, rb_f), (tb_p, rb_p), (tb_c, rb_c, rs_c) = tiles
    h1 = _ffn_call(x, mod, 0, w['g1'], w['ff1_in'], w['ff1_out'], tb_f, rb_f)[0]
    q, kvp, kvpb, kvw, kvwb, gates, u, sga, sgc = _proj_call(h1, mod, w['g2'], w['proj'], tb_p, rb_p)
    conv = _conv_call(u, conv_hist, w['w_dw'], w['b_dw'], w['g_cln'], w['b_cln'], tb_c, rb_c, rs_c)
    attn = attn_fn(q, kvp, kvpb, kvw, kvwb, gates)
    h2 = _merge_call(h1, attn, conv, sga, sgc, mod, w['br_attn'], w['br_conv'], w['out'], tb_f, rb_f)
    _, y = _ffn_call(h2, mod, 6, w['g3'], w['ff2_in'], w['ff2_out'], tb_f, rb_f, gf=w['g_final'])
    return y, kvp, kvw, u


def kernel(x_prompt, x_sample, c_prompt, c_sample, cache_kv, state_win, state_conv, page_table, w_ada, b_ada,
           g_norm1, g_norm2, g_norm3, g_final, w_ff1_in, w_ff1_out, w_ff2_in, w_ff2_out, w_in, pe_cmp, w_phi1,
           b_phi1, w_phi2, w_dw, b_dw, g_conv_ln, b_conv_ln, w_br_attn, w_br_conv, w_out):
    assert w_ada.shape[0] == 1 and cache_kv.shape[2] == 1, "single layer"
    b, t_len, d = x_prompt.shape
    db, ds, _ = x_sample.shape
    wb = state_win.shape[1]
    cw1 = CONV_WIDTH - 1
    q_w = N_HEADS * HEAD_DIM
    kvp_w = KV_HEADS * 4 * HEAD_DIM
    kvw_w = KV_HEADS * 2 * HEAD_DIM
    ng_w = 3 * N_HEADS
    c_dim = w_dw.shape[2]
    assert t_len >= wb and t_len >= cw1

    wi = w_in[0]
    o0, o1, o2, o3, o4 = np.cumsum([q_w, kvp_w, kvw_w, ng_w, 2 * c_dim])
    perm = np.array([r * N_HEADS + g * HPG + h for g in range(KV_HEADS) for r in range(3) for h in range(HPG)])
    w_ng = jnp.pad(wi[:, o2:o3][:, perm], ((0, 0), (0, LANES - ng_w)))
    proj = tuple(v.astype(BF) for v in (wi[:, :o0] * HEAD_DIM ** -0.5, wi[:, o0:o1], wi[:, o1:o2], w_ng,
                                        wi[:, o3:o4], wi[:, o4:]))
    w = dict(g1=g_norm1[0], g2=g_norm2[0], g3=g_norm3[0], g_final=g_final,
             ff1_in=w_ff1_in[0].astype(BF), ff1_out=w_ff1_out[0].astype(BF),
             ff2_in=w_ff2_in[0].astype(BF), ff2_out=w_ff2_out[0].astype(BF), proj=proj,
             w_dw=w_dw[0], b_dw=b_dw[0], g_cln=g_conv_ln[0], b_cln=b_conv_ln[0],
             br_attn=w_br_attn[0].astype(BF), br_conv=w_br_conv[0].astype(BF), out=w_out[0].astype(BF))
    wbig, w2 = _cmp_weights(w_phi1[0], w_phi2[0])
    hb = _cmp_bias_call(pe_cmp[0], wbig, b_phi1[0])

    nseq = b + db
    pad = -nseq % 8
    c_all = jnp.concatenate([c_prompt, c_sample, jnp.zeros((pad, d), F32)], axis=0)
    mod = _mod_call(c_all, w_ada[0], b_ada[0])
    mod_p = mod[:b].reshape(b, 1, N_MOD * d)
    mod_s = mod[b:nseq].reshape(db, 1, N_MOD * d)

    attn_p = lambda q, kvp, kvpb, kvw, kvwb, gates: _nsa_prompt_call(q, kvp, kvpb, kvwb, gates, wbig, hb, w2)
    y_p, kvp_p, kvw_p, u_p = _layer(x_prompt, mod_p, attn_p, None, w,
                                    ((1, 512), (1, 256), (1, 256, 32)))

    cache_t = jnp.transpose(cache_kv, (0, 2, 3, 4, 5, 1)).reshape(cache_kv.shape[0], kvp_w, PAGE_SIZE)
    win_t = jnp.transpose(state_win, (0, 2, 3, 4, 5, 1)).reshape(db, kvw_w, wb)
    attn_s = lambda q, kvp, kvpb, kvw, kvwb, gates: _nsa_sample_call(page_table, q, kvp, kvw, win_t, gates,
                                                                      cache_t, wbig, hb, w2)
    sb = min(db, 64)
    y_s, kvp_s, kvw_s, u_s = _layer(x_sample, mod_s, attn_s, state_conv[:, :, 0], w,
                                    ((sb, ds), (min(db, 32), ds), (min(db, 16), ds, ds)))

    kv5 = lambda a: a.reshape(a.shape[0], a.shape[1], 1, KV_HEADS, 4, HEAD_DIM)
    win5 = lambda a: a.reshape(a.shape[0], a.shape[1], 1, KV_HEADS, 2, HEAD_DIM)
    new_win_p = win5(kvw_p[:, t_len - wb:])
    new_win_s = jnp.concatenate([state_win, win5(kvw_s)], axis=1)[:, -wb:]
    new_conv_p = u_p[:, t_len - cw1:][:, :, None, :]
    new_conv_s = jnp.concatenate([state_conv[:, :, 0], u_s], axis=1)[:, -cw1:][:, :, None, :]
    return (y_p, y_s, kv5(kvp_p), kv5(kvp_s), new_win_p, new_win_s, new_conv_p, new_conv_s)
```

```python
import functools

import numpy as np
import jax
import jax.numpy as jnp
from jax import lax
from jax.experimental import pallas as pl
from jax.experimental.pallas import tpu as pltpu

F32 = jnp.float32
BF = jnp.bfloat16

N_HEADS = 16
HEAD_DIM = 64
KV_HEADS = 4
HPG = N_HEADS // KV_HEADS
CMP_LEN = 32
CMP_STRIDE = 16
SEL_BLOCK = 64
N_SEL = 16
WINDOW = 512
PHI_HID = 128
CONV_WIDTH = 31
N_MOD = 9
EPS = 1e-6
NEG = -1e30
FORCE = 1e4
PAGE_SIZE = 128

LANES = 128
KV_LANES = 2 * HEAD_DIM
Q_LANES = HPG * HEAD_DIM
N_GATE = 3 * HPG
N_POS = 6
VMEM_LIMIT = 56 * 1024 * 1024


def _cparams(sem):
    return pltpu.CompilerParams(dimension_semantics=sem, vmem_limit_bytes=VMEM_LIMIT)


def _const_spec(shape):
    nd = len(shape)
    return pl.BlockSpec(shape, lambda *_: (0,) * nd, pipeline_mode=pl.Buffered(1))


def _tok(ref):
    tb, rb, w = ref.shape
    return ref[0] if tb == 1 else ref[...].reshape(tb * rb, w)


def _modrow(ref, rb):
    tb, _, w = ref.shape
    if tb == 1:
        return ref[0]
    return jnp.broadcast_to(ref[...], (tb, rb, w)).reshape(tb * rb, w)


def _put(ref, val):
    tb, rb, w = ref.shape
    if tb == 1:
        ref[0] = val.astype(ref.dtype)
    else:
        ref[...] = val.reshape(tb, rb, w).astype(ref.dtype)


def _rms(x, g):
    return x * lax.rsqrt(jnp.mean(x * x, axis=-1, keepdims=True) + EPS) * g


def _silu(x):
    return x * jax.nn.sigmoid(x)


def _mm(a, b):
    return jnp.dot(a, b, preferred_element_type=F32)


def _mm_t(a, b):
    return lax.dot_general(a, b, (((1,), (1,)), ((), ())), preferred_element_type=F32)


def _tok_spec(tb, rb, w):
    return pl.BlockSpec((tb, rb, w), lambda i, j: (i, j, 0))


def _mod_spec(tb, d, col):
    return pl.BlockSpec((tb, 1, d), lambda i, j: (i, 0, col))


def _mod_kernel(c_ref, w_ref, b_ref, o_ref):
    a = _silu(c_ref[...]).astype(BF)
    o_ref[...] = _mm(a, w_ref[...].astype(BF)) + b_ref[...]


def _mod_call(c, w, b):
    m, d = c.shape
    n = w.shape[1]
    tn = d
    return pl.pallas_call(
        _mod_kernel,
        grid=(n // tn,),
        in_specs=[pl.BlockSpec((m, d), lambda j: (0, 0)),
                  pl.BlockSpec((d, tn), lambda j: (0, j)),
                  pl.BlockSpec((1, tn), lambda j: (0, j))],
        out_specs=pl.BlockSpec((m, tn), lambda j: (0, j)),
        out_shape=jax.ShapeDtypeStruct((m, n), F32),
        compiler_params=_cparams(("arbitrary",)),
    )(c, w, b.reshape(1, n))


def _ffn_kernel(*refs, d_ff, fc, final):
    if final:
        x_ref, sh_ref, sc_ref, gt_ref, g_ref, wi_ref, wo_ref, gf_ref, h_ref, y_ref = refs
    else:
        x_ref, sh_ref, sc_ref, gt_ref, g_ref, wi_ref, wo_ref, h_ref = refs
    rb = x_ref.shape[1]
    x = _tok(x_ref)
    n = (_rms(x, g_ref[...]) * (1.0 + _modrow(sc_ref, rb)) + _modrow(sh_ref, rb)).astype(BF)
    acc = jnp.zeros(x.shape, F32)
    for c in range(d_ff // fc):
        gate = _mm(n, wi_ref[:, c * fc:(c + 1) * fc])
        up = _mm(n, wi_ref[:, d_ff + c * fc:d_ff + (c + 1) * fc])
        acc = acc + _mm((_silu(gate) * up).astype(BF), wo_ref[c * fc:(c + 1) * fc, :])
    h = x + _modrow(gt_ref, rb) * (0.5 * acc)
    _put(h_ref, h)
    if final:
        _put(y_ref, _rms(h, gf_ref[...]))


def _ffn_call(x, mod, col0, g, wi, wo, tb, rb, gf=None):
    nb, r, d = x.shape
    d_ff = wo.shape[0]
    fc = d_ff // 2
    final = gf is not None
    ins = [x, mod, mod, mod, g.reshape(1, d), wi, wo]
    specs = [_tok_spec(tb, rb, d), _mod_spec(tb, d, col0), _mod_spec(tb, d, col0 + 1),
             _mod_spec(tb, d, col0 + 2), _const_spec((1, d)), _const_spec(wi.shape), _const_spec(wo.shape)]
    out_shape = [jax.ShapeDtypeStruct(x.shape, F32)]
    out_specs = [_tok_spec(tb, rb, d)]
    if final:
        ins.append(gf.reshape(1, d))
        specs.append(_const_spec((1, d)))
        out_shape.append(jax.ShapeDtypeStruct(x.shape, F32))
        out_specs.append(_tok_spec(tb, rb, d))
    return pl.pallas_call(
        functools.partial(_ffn_kernel, d_ff=d_ff, fc=fc, final=final),
        grid=(nb // tb, r // rb),
        in_specs=specs, out_specs=out_specs, out_shape=out_shape,
        compiler_params=_cparams(("arbitrary", "arbitrary")),
    )(*ins)


def _proj_kernel(h_ref, sh_ref, sc_ref, g_ref, wq_ref, wkvp_ref, wkvw_ref, wng_ref, wglu_ref, wmg_ref,
                 q_ref, kvp_ref, kvpb_ref, kvw_ref, kvwb_ref, gates_ref, u_ref, sga_ref, sgc_ref):
    tb, rb, d = h_ref.shape
    n = (_rms(_tok(h_ref), g_ref[...]) * (1.0 + _modrow(sc_ref, rb)) + _modrow(sh_ref, rb)).astype(BF)
    _put(q_ref, _mm(n, wq_ref[...]))
    kvp = _mm(n, wkvp_ref[...])
    _put(kvp_ref, kvp)
    _put(kvpb_ref, kvp)
    kvw = _mm(n, wkvw_ref[...])
    _put(kvw_ref, kvw)
    _put(kvwb_ref, kvw)
    ng = jax.nn.sigmoid(_mm(n, wng_ref[...]))
    for g in range(KV_HEADS):
        gates_ref[g] = ng[:, g * N_GATE:(g + 1) * N_GATE].reshape(tb, rb, N_GATE)
    glu = _mm(n, wglu_ref[...])
    c = glu.shape[1] // 2
    _put(u_ref, glu[:, :c] * jax.nn.sigmoid(glu[:, c:]))
    mg = jax.nn.sigmoid(_mm(n, wmg_ref[...]))
    _put(sga_ref, mg[:, :d])
    _put(sgc_ref, mg[:, d:])


def _proj_call(h, mod, g, ws, tb, rb):
    nb, r, d = h.shape
    wq, wkvp, wkvw, wng, wglu, wmg = ws
    c = wglu.shape[1] // 2
    widths = [(wq.shape[1], BF), (wkvp.shape[1], F32), (wkvp.shape[1], BF), (wkvw.shape[1], F32),
              (wkvw.shape[1], BF)]
    out_shape = [jax.ShapeDtypeStruct((nb, r, w), dt) for w, dt in widths]
    out_specs = [_tok_spec(tb, rb, w) for w, _ in widths]
    out_shape.append(jax.ShapeDtypeStruct((KV_HEADS, nb, r, N_GATE), F32))
    out_specs.append(pl.BlockSpec((KV_HEADS, tb, rb, N_GATE), lambda i, j: (0, i, j, 0)))
    for w in (c, d, d):
        out_shape.append(jax.ShapeDtypeStruct((nb, r, w), F32))
        out_specs.append(_tok_spec(tb, rb, w))
    return pl.pallas_call(
        _proj_kernel,
        grid=(nb // tb, r // rb),
        in_specs=[_tok_spec(tb, rb, d), _mod_spec(tb, d, 3), _mod_spec(tb, d, 4), _const_spec((1, d))]
                 + [_const_spec(w.shape) for w in ws],
        out_specs=out_specs, out_shape=out_shape,
        compiler_params=_cparams(("arbitrary", "arbitrary")),
    )(h, mod, mod, g.reshape(1, d), *ws)


CONV_HIST = 32


def _conv_kernel(u_ref, hist_ref, w_ref, b_ref, g_ref, bl_ref, o_ref, sh_ref, *, zero_first, rs):
    tb, rb, c = u_ref.shape
    hr = hist_ref.shape[1]
    seg = CONV_HIST + rb
    hist = hist_ref[...]
    if zero_first:
        hist = jnp.where(pl.program_id(1) > 0, hist, 0.0)
    for s in range(tb):
        if hr < CONV_HIST:
            sh_ref[0, s * seg:s * seg + CONV_HIST - hr, :] = jnp.zeros((CONV_HIST - hr, c), F32)
        sh_ref[0, s * seg + CONV_HIST - hr:s * seg + CONV_HIST, :] = hist[s]
        sh_ref[0, s * seg + CONV_HIST:(s + 1) * seg, :] = u_ref[s]
    ext = sh_ref[0]
    for b in range(1, 8):
        sh_ref[b] = pltpu.roll(ext, tb * seg - b, 0)
    first = CONV_HIST - (CONV_WIDTH - 1)
    w = w_ref[...]
    for s in range(tb):
        for r0 in range(0, rb, rs):
            acc = jnp.broadcast_to(b_ref[...], (rs, c))
            for k in range(CONV_WIDTH):
                a, b = divmod(k + first, 8)
                row = s * seg + r0 + 8 * a
                acc = acc + w[k:k + 1, :] * sh_ref[b, row:row + rs, :]
            xc = acc - jnp.mean(acc, axis=-1, keepdims=True)
            y = xc * lax.rsqrt(jnp.mean(xc * xc, axis=-1, keepdims=True) + EPS) * g_ref[...] + bl_ref[...]
            o_ref[s, r0:r0 + rs, :] = _silu(y).astype(o_ref.dtype)


def _conv_call(u, hist, w, b, g, bl, tb, rb, rs):
    nb, r, c = u.shape
    row = lambda v: v.reshape(1, c)
    if hist is None:
        hist_arr, hr = u, CONV_HIST
        hspec = pl.BlockSpec((tb, hr, c), lambda i, j: (i, jnp.maximum(j * (rb // CONV_HIST) - 1, 0), 0))
    else:
        hist_arr, hr = hist, hist.shape[1]
        hspec = pl.BlockSpec((tb, hr, c), lambda i, j: (i, 0, 0))
    return pl.pallas_call(
        functools.partial(_conv_kernel, zero_first=hist is None, rs=rs),
        grid=(nb // tb, r // rb),
        in_specs=[_tok_spec(tb, rb, c), hspec, _const_spec(w.shape)] + [_const_spec((1, c))] * 3,
        out_specs=_tok_spec(tb, rb, c),
        out_shape=jax.ShapeDtypeStruct(u.shape, BF),
        scratch_shapes=[pltpu.VMEM((8, tb * (CONV_HIST + rb), c), F32)],
        compiler_params=_cparams(("arbitrary", "arbitrary")),
    )(u, hist_arr, w, row(b), row(g), row(bl))


def _merge_kernel(h_ref, attn_ref, conv_ref, sga_ref, sgc_ref, gt_ref, wa_ref, wc_ref, wo_ref, o_ref):
    rb = h_ref.shape[1]
    m = _tok(sga_ref) * _mm(_tok(attn_ref), wa_ref[...]) + _tok(sgc_ref) * _mm(_tok(conv_ref), wc_ref[...])
    _put(o_ref, _tok(h_ref) + _modrow(gt_ref, rb) * _mm(m.astype(BF), wo_ref[...]))


def _merge_call(h, attn, conv, sga, sgc, mod, wa, wc, wo, tb, rb):
    nb, r, d = h.shape
    return pl.pallas_call(
        _merge_kernel,
        grid=(nb // tb, r // rb),
        in_specs=[_tok_spec(tb, rb, d), _tok_spec(tb, rb, attn.shape[2]), _tok_spec(tb, rb, conv.shape[2]),
                  _tok_spec(tb, rb, d), _tok_spec(tb, rb, d), _mod_spec(tb, d, 5),
                  _const_spec(wa.shape), _const_spec(wc.shape), _const_spec(wo.shape)],
        out_specs=_tok_spec(tb, rb, d),
        out_shape=jax.ShapeDtypeStruct(h.shape, F32),
        compiler_params=_cparams(("arbitrary", "arbitrary")),
    )(h, attn, conv, sga, sgc, mod, wa, wc, wo)


def _cmp_weights(w_phi1, w_phi2):
    c = CMP_LEN // CMP_STRIDE
    w = w_phi1.reshape(2, c, CMP_STRIDE, HEAD_DIM, PHI_HID)
    eye = jnp.eye(2, dtype=w.dtype)
    wbig = jnp.einsum('kosdh,kc->skdcoh', w, eye).reshape(CMP_STRIDE * KV_LANES, 2 * c * PHI_HID)
    w2 = jnp.einsum('khd,kc->khcd', w_phi2, eye).reshape(2 * PHI_HID, KV_LANES)
    return wbig.astype(BF), w2.astype(BF)


def _cmp_bias_kernel(pe_ref, wbig_ref, b_ref, o_ref):
    res = _mm(pe_ref[...].astype(BF), wbig_ref[...])
    hs = [b_ref[:, k * PHI_HID:(k + 1) * PHI_HID]
          + res[0:1, 2 * k * PHI_HID:(2 * k + 1) * PHI_HID]
          + res[1:2, (2 * k + 1) * PHI_HID:(2 * k + 2) * PHI_HID] for k in range(2)]
    o_ref[...] = jnp.concatenate(hs, axis=1)


def _cmp_bias_call(pe_cmp, wbig, b_phi1):
    c = CMP_LEN // CMP_STRIDE
    pe = pe_cmp.reshape(2, c, CMP_STRIDE, HEAD_DIM).transpose(1, 2, 0, 3).reshape(c, CMP_STRIDE * KV_LANES)
    pe = jnp.concatenate([pe, jnp.zeros((8 - c, pe.shape[1]), pe.dtype)], axis=0)
    return pl.pallas_call(
        _cmp_bias_kernel,
        out_shape=jax.ShapeDtypeStruct((1, 2 * PHI_HID), F32),
        compiler_params=pltpu.CompilerParams(vmem_limit_bytes=VMEM_LIMIT),
    )(pe, wbig, b_phi1.reshape(1, 2 * PHI_HID))


def _compress(lhs, wbig_ref, hb_ref, w2_ref):
    part = _mm(lhs, wbig_ref[...])
    n = part.shape[0]
    nxt = pltpu.roll(part, n - 1, 0)
    hb = hb_ref[...]
    hs = [hb[:, k * PHI_HID:(k + 1) * PHI_HID]
          + part[:, 2 * k * PHI_HID:(2 * k + 1) * PHI_HID]
          + nxt[:, (2 * k + 1) * PHI_HID:(2 * k + 2) * PHI_HID] for k in range(2)]
    h = jnp.concatenate(hs, axis=1)
    return _mm(_silu(h).astype(BF), w2_ref[...])


def _stack_q(q, r):
    q = q.astype(F32)
    return jnp.concatenate([q[:, h * HEAD_DIM:(h + 1) * HEAD_DIM] for h in range(HPG)], axis=0)


def _row_info(r, g, t0):
    row = lax.broadcasted_iota(jnp.int32, (HPG * r, 1), 0)
    t = t0 + (row & (r - 1))
    head = g * HPG + lax.shift_right_logical(row, r.bit_length() - 1)
    slope = jnp.exp2(-(8.0 / N_HEADS) * (head + 1).astype(F32))
    return t, slope


def _q_aux(slope, width, pos0, selneg=None):
    hi = slope.astype(BF).astype(F32)
    rest = slope - hi
    mid = rest.astype(BF).astype(F32)
    lo = rest - mid
    lane = lax.broadcasted_iota(jnp.int32, (slope.shape[0], width), 1) - pos0
    base = jnp.zeros((slope.shape[0], width), F32) if selneg is None else selneg
    aux = jnp.where((lane == 0) | (lane == 3), hi, base)
    aux = jnp.where((lane == 1) | (lane == 4), mid, aux)
    aux = jnp.where((lane == 2) | (lane == 5), lo, aux)
    return aux.astype(BF)


def _k_aux(pos, blocks, width, pos0):
    pos = np.asarray(pos)
    a = np.zeros((pos.shape[0], width), np.float32)
    if blocks is not None:
        a[np.arange(pos.shape[0]), np.asarray(blocks)] = 1.0
    a[:, pos0:pos0 + 3] = (pos // SEL_BLOCK * SEL_BLOCK)[:, None]
    a[:, pos0 + 3:pos0 + 6] = (pos % SEL_BLOCK)[:, None]
    return a


def _topk_mask(xt, k):
    nb = xt.shape[0]
    rowf = lax.broadcasted_iota(jnp.int32, xt.shape, 0).astype(F32)

    def body(_, carry):
        x, sel = carry
        m = jnp.max(x, axis=0, keepdims=True)
        idx = jnp.min(jnp.where(x == m, rowf, float(nb)), axis=0, keepdims=True)
        pick = rowf == idx
        return jnp.where(pick, -3.0, x), jnp.where(pick, 1.0, sel)

    _, sel = lax.fori_loop(0, k, body, (xt, jnp.zeros_like(xt)))
    return sel


def _select(score, ns):
    nbp = score.shape[1]
    nb8 = -(-ns // 8) * 8
    sel_t = _topk_mask(score.T[0:nb8], N_SEL)
    if nb8 < nbp:
        sel_t = jnp.concatenate([sel_t, jnp.zeros((nbp - nb8, sel_t.shape[1]), F32)], axis=0)
    return sel_t.T


def _cmp_branch(qx, kvc, caux_ref, t, r, n_cmp, ns, m_ref):
    nc = kvc.shape[0]
    nbp = m_ref.shape[1]
    ci = lax.broadcasted_iota(jnp.int32, (1, nc), 1)
    mask = (t >= ci * CMP_STRIDE + (CMP_LEN - 1)) & (ci < n_cmp)
    s = jnp.where(mask, _mm_t(qx, jnp.concatenate([kvc, caux_ref[...]], axis=1)), NEG)
    e = jnp.where(mask, jnp.exp(s - jnp.max(s, axis=-1, keepdims=True)), 0.0)
    p = e / jnp.maximum(jnp.sum(e, axis=-1, keepdims=True), 1e-30)
    o_c = _mm(p.astype(BF), kvc)
    psum = p[0:r] + p[r:2 * r] + p[2 * r:3 * r] + p[3 * r:4 * r]
    imp = jnp.dot(psum, m_ref[...], precision=lax.Precision.HIGHEST, preferred_element_type=F32)
    blk = lax.broadcasted_iota(jnp.int32, (1, nbp), 1)
    cur = lax.shift_right_logical(t[0:r], SEL_BLOCK.bit_length() - 1)
    forced = (blk == 0) | (blk == cur) | (blk == cur - 1)
    score = jnp.where(blk <= cur, imp + jnp.where(forced, FORCE, 0.0), -1.0)
    return o_c, jnp.where(blk < ns, score, -2.0)


def _ones_v(kv):
    return jnp.where(lax.broadcasted_iota(jnp.int32, kv.shape, 1) < HEAD_DIM, jnp.ones_like(kv), kv)


def _gate_col(gt, branch):
    return jnp.concatenate([gt[:, branch * HPG + h:branch * HPG + h + 1] for h in range(HPG)], axis=0)


def _unstack(o, r):
    return jnp.concatenate([o[h * r:(h + 1) * r] for h in range(HPG)], axis=1)


def _imp_matrix(nc, n_cmp, nbp, ns):
    i = np.arange(nc)[:, None]
    j = np.arange(nbp)[None, :]
    r = SEL_BLOCK // CMP_STRIDE
    c = CMP_LEN // CMP_STRIDE
    return jnp.asarray((i >= r * j - (c - 1)) & (i <= r * j + r - 1) & (i < n_cmp) & (j < ns), F32)


def _nsa_prompt_kernel(q_ref, kc_ref, ks_ref, kw_ref, gt_ref, wbig_ref, hb_ref, w2_ref, kaux_ref, caux_ref,
                       m_ref, o_ref, kvc_ref, *, tq, t_len, ck, ns):
    g = pl.program_id(1)
    i = pl.program_id(2)
    n = t_len // CMP_STRIDE
    rows = HPG * tq

    @pl.when(i == 0)
    def _():
        lhs = jnp.concatenate([kc_ref[0, pl.ds(s, n, stride=CMP_STRIDE), :].astype(BF)
                               for s in range(CMP_STRIDE)], axis=1)
        kvc = _compress(lhs, wbig_ref, hb_ref, w2_ref)
        kvc_ref[...] = kvc.astype(BF)

    start = i * tq
    t, slope = _row_info(tq, g, start)
    qs = _stack_q(q_ref[0], tq)
    qk = jnp.concatenate([qs, jnp.zeros_like(qs)], axis=1).astype(BF)
    qx = jnp.concatenate([qk, _q_aux(slope, LANES, ns)], axis=1)

    o_c, score = _cmp_branch(qx, kvc_ref[...], caux_ref, t, tq, n - 1, ns, m_ref)
    sel = _select(score, ns)

    lo = pl.multiple_of(jnp.maximum(start - WINDOW, 0), tq)
    kw = kw_ref[0, pl.ds(lo, WINDOW + tq), :]
    s = _mm_t(qx, jnp.concatenate([kw, kaux_ref[pl.ds(lo, WINDOW + tq), :]], axis=1))
    dist = t - (lo + lax.broadcasted_iota(jnp.int32, (1, WINDOW + tq), 1))
    s = jnp.where((dist >= 0) & (dist <= WINDOW), s, NEG)
    e = jnp.exp(s - jnp.max(s, axis=-1, keepdims=True)).astype(BF)
    o_w = _mm(e, _ones_v(kw))
    o_w = o_w / o_w[:, 0:1]

    selneg = jnp.concatenate([(sel - 1.0) * (-NEG)] * HPG, axis=0)
    qx = jnp.concatenate([qk, _q_aux(slope, LANES, ns, selneg)], axis=1)

    def chunk(c, carry, causal):
        m, acc = carry
        k0 = pl.multiple_of(c * ck, ck)
        kv = ks_ref[0, pl.ds(k0, ck), :]
        s = _mm_t(qx, jnp.concatenate([kv, kaux_ref[pl.ds(k0, ck), :]], axis=1))
        if causal:
            s = jnp.where(t >= k0 + lax.broadcasted_iota(jnp.int32, (1, ck), 1), s, NEG)
        m_new = jnp.maximum(m, jnp.max(s, axis=-1, keepdims=True))
        e = jnp.exp(s - m_new).astype(BF)
        acc = jnp.exp(m - m_new) * acc + _mm(e, _ones_v(kv))
        return m_new, acc

    init = (jnp.full((rows, 1), NEG, F32), jnp.zeros((rows, KV_LANES), F32))
    last = (start + tq - 1) // ck
    carry = lax.fori_loop(0, last, lambda c, cr: chunk(c, cr, False), init)
    _, acc = chunk(last, carry, True)
    o_s = acc / acc[:, 0:1]

    gt = gt_ref[0, 0]
    o = _gate_col(gt, 0) * o_c + _gate_col(gt, 1) * o_s + _gate_col(gt, 2) * o_w
    o_ref[0] = _unstack(o[:, HEAD_DIM:], tq).astype(o_ref.dtype)


def _nsa_prompt_call(q, kvp, kvpb, kvwb, gates, wbig, hb, w2, tq=128, ck=512):
    b, t_len, _ = q.shape
    nc = t_len // CMP_STRIDE
    ns = t_len // SEL_BLOCK
    assert t_len % ck == 0 and t_len >= WINDOW + tq and N_SEL <= ns <= LANES - N_POS and tq == LANES
    pos = np.arange(t_len)
    kaux = jnp.asarray(_k_aux(pos, pos // SEL_BLOCK, LANES, ns), BF)
    caux = jnp.asarray(_k_aux(np.arange(nc) * CMP_STRIDE + CMP_LEN - 1, None, LANES, ns), BF)
    m_mat = _imp_matrix(nc, nc - 1, LANES, ns)
    return pl.pallas_call(
        functools.partial(_nsa_prompt_kernel, tq=tq, t_len=t_len, ck=ck, ns=ns),
        grid=(b, KV_HEADS, t_len // tq),
        in_specs=[pl.BlockSpec((1, tq, Q_LANES), lambda b, g, i: (b, i, g)),
                  pl.BlockSpec((1, t_len, KV_LANES), lambda b, g, i: (b, 0, 2 * g)),
                  pl.BlockSpec((1, t_len, KV_LANES), lambda b, g, i: (b, 0, 2 * g + 1)),
                  pl.BlockSpec((1, t_len, KV_LANES), lambda b, g, i: (b, 0, g)),
                  pl.BlockSpec((1, 1, tq, N_GATE), lambda b, g, i: (g, b, i, 0)),
                  _const_spec(wbig.shape), _const_spec(hb.shape), _const_spec(w2.shape),
                  _const_spec(kaux.shape), _const_spec(caux.shape), _const_spec(m_mat.shape)],
        out_specs=pl.BlockSpec((1, tq, Q_LANES), lambda b, g, i: (b, i, g)),
        out_shape=jax.ShapeDtypeStruct(q.shape, BF),
        scratch_shapes=[pltpu.VMEM((nc, KV_LANES), BF)],
        compiler_params=_cparams(("arbitrary", "arbitrary", "arbitrary")),
    )(q, kvp, kvpb, kvwb, gates, wbig, hb, w2, kaux, caux, m_mat)


NEW_PAD = 128


def _page_stream(pt_ref, cache_ref, sem, row0, n_pages, dst_of):
    b = pl.program_id(0)
    g = pl.program_id(1)
    step = b * KV_HEADS + g
    total = pl.num_programs(0) * KV_HEADS
    slot = step % 2

    def copy(bb, gg, p, sl):
        rows = pl.ds(pl.multiple_of(gg * 2 * KV_LANES + row0, KV_LANES), KV_LANES)
        return pltpu.make_async_copy(cache_ref.at[pt_ref[bb, p], rows, :], dst_of(sl, p), sem.at[sl])

    def fetch(st, sl):
        bb = st // KV_HEADS
        gg = st % KV_HEADS

        def body(p, carry):
            copy(bb, gg, p, sl).start()
            return carry

        lax.fori_loop(0, n_pages, body, 0, unroll=4)

    @pl.when(step == 0)
    def _():
        fetch(step, slot)

    @pl.when(step + 1 < total)
    def _():
        fetch(step + 1, 1 - slot)

    def wait_body(p, carry):
        copy(b, g, p, slot).wait()
        return carry

    lax.fori_loop(0, n_pages, wait_body, 0, unroll=4)
    return slot


def _sample_cmp_kernel(pt_ref, q_ref, cache_ref, perm_ref, wbig_ref, hb_ref, w2_ref, caux_ref, m_ref,
                       oc_ref, score_ref, buf_ref, lhs_ref, sem, *, past, ds, ns):
    g = pl.program_id(1)
    n_pages = past // PAGE_SIZE
    slot = _page_stream(pt_ref, cache_ref, sem, 0, n_pages, lambda sl, p: buf_ref.at[sl, p])

    cpp = PAGE_SIZE // CMP_STRIDE
    perm = perm_ref[...]

    def regroup(i, carry):
        res = [_mm_t(perm, buf_ref[slot, 2 * i + j].astype(BF)) for j in range(2)]
        r0 = pl.multiple_of(i * 2 * cpp, 2 * cpp)
        for s in range(CMP_STRIDE):
            rows = jnp.concatenate([r[s * cpp:(s + 1) * cpp] for r in res], axis=0)
            lhs_ref[pl.ds(r0, 2 * cpp), s * KV_LANES:(s + 1) * KV_LANES] = rows.astype(BF)
        return carry

    lax.fori_loop(0, n_pages // 2, regroup, 0, unroll=min(16, n_pages // 2))
    n = past // CMP_STRIDE
    kvc = _compress(lhs_ref[...], wbig_ref, hb_ref, w2_ref).astype(BF)

    t, slope = _row_info(ds, g, past)
    qs = _stack_q(q_ref[0], ds)
    qx = jnp.concatenate([qs, jnp.zeros_like(qs)], axis=1).astype(BF)
    qx = jnp.concatenate([qx, _q_aux(slope, LANES, 0)], axis=1)
    o_c, score = _cmp_branch(qx, kvc, caux_ref, t, ds, n - 1, ns, m_ref)
    oc_ref[0, 0] = o_c
    score_ref[0, 0] = score


def _topk_kernel(score_ref, sel_ref, *, ns):
    sel_ref[...] = _select(score_ref[...], ns)


def _sample_sel_kernel(pt_ref, q_ref, kvpn_ref, kvwn_ref, win_ref, gt_ref, oc_ref, sel_ref, cache_ref, eaux_ref,
                       o_ref, buf_ref, sem, *, past, ds, wb, ns):
    g = pl.program_id(1)
    n_pages = past // PAGE_SIZE
    slot = _page_stream(pt_ref, cache_ref, sem, KV_LANES, n_pages,
                        lambda sl, p: buf_ref.at[sl, :, pl.ds(pl.multiple_of(p * PAGE_SIZE, PAGE_SIZE), PAGE_SIZE)])

    t, slope = _row_info(ds, g, past)
    qs = _stack_q(q_ref[0], ds).astype(BF)
    zpad = jnp.zeros((NEW_PAD - ds, KV_LANES), F32)
    lane = lax.broadcasted_iota(jnp.int32, (1, NEW_PAD), 1)

    def new_t(rows):
        x = jnp.concatenate([rows, zpad], axis=0).T.astype(BF)
        return x[0:HEAD_DIM], x[HEAD_DIM:KV_LANES]

    naux = eaux_ref.shape[0]
    selneg = jnp.concatenate([(sel_ref[0, 0][:, 0:naux] - 1.0) * (-NEG)] * HPG, axis=0)
    qa = _q_aux(slope, naux, ns, selneg)
    kn, vn = new_t(kvpn_ref[0][:, KV_LANES:])
    kt = buf_ref[slot, 0:HEAD_DIM, :].astype(BF)
    vt = buf_ref[slot, HEAD_DIM:KV_LANES, :].astype(BF)
    s_p = _mm(qs, kt) + _mm(qa, eaux_ref[:, 0:past])
    s_n = _mm(qs, kn) + _mm(qa, eaux_ref[:, past:past + NEW_PAD])
    s_n = jnp.where(t >= past + lane, s_n, NEG)
    m = jnp.maximum(jnp.max(s_p, axis=-1, keepdims=True), jnp.max(s_n, axis=-1, keepdims=True))
    e_p = jnp.exp(s_p - m)
    e_n = jnp.exp(s_n - m)
    l = jnp.sum(e_p, axis=-1, keepdims=True) + jnp.sum(e_n, axis=-1, keepdims=True)
    o_s = (_mm_t(e_p.astype(BF), vt) + _mm_t(e_n.astype(BF), vn)) / l

    kn, vn = new_t(kvwn_ref[0])
    kt = win_ref[0, 0:HEAD_DIM, :].astype(BF)
    vt = win_ref[0, HEAD_DIM:KV_LANES, :].astype(BF)

    def band(s, kpos):
        dist = t - kpos
        return jnp.where((dist >= 0) & (dist <= WINDOW) & (kpos >= 0), s - slope * dist.astype(F32), NEG)

    s_p = band(_mm(qs, kt), (past - wb) + lax.broadcasted_iota(jnp.int32, (1, wb), 1))
    s_n = band(_mm(qs, kn), past + lane)
    m = jnp.maximum(jnp.max(s_p, axis=-1, keepdims=True), jnp.max(s_n, axis=-1, keepdims=True))
    e_p = jnp.exp(s_p - m)
    e_n = jnp.exp(s_n - m)
    l = jnp.sum(e_p, axis=-1, keepdims=True) + jnp.sum(e_n, axis=-1, keepdims=True)
    o_w = (_mm_t(e_p.astype(BF), vt) + _mm_t(e_n.astype(BF), vn)) / l

    gt = gt_ref[0, 0]
    o = _gate_col(gt, 0) * oc_ref[0, 0][:, HEAD_DIM:] + _gate_col(gt, 1) * o_s + _gate_col(gt, 2) * o_w
    o_ref[0] = _unstack(o, ds).astype(o_ref.dtype)


def _nsa_sample_call(page_table, q, kvp, kvw, win_t, gates, cache_t, wbig, hb, w2):
    db, ds, _ = q.shape
    n_pages = page_table.shape[1]
    past = n_pages * PAGE_SIZE
    wb = win_t.shape[2]
    assert ds < CMP_STRIDE and ds & (ds - 1) == 0 and ds % 8 == 0
    nc = past // CMP_STRIDE
    ns = -(-(past + ds) // SEL_BLOCK)
    nbp = -(-(ns + N_POS) // LANES) * LANES
    rows = HPG * ds
    caux = jnp.asarray(_k_aux(np.arange(nc) * CMP_STRIDE + CMP_LEN - 1, None, LANES, 0), BF)
    m_mat = _imp_matrix(nc, nc - 1, nbp, ns)
    pos = np.arange(past + NEW_PAD)
    naux = -(-(ns + N_POS) // 16) * 16
    eaux = jnp.asarray(_k_aux(pos, np.minimum(pos // SEL_BLOCK, ns), naux, ns).T, BF)
    key = np.arange(PAGE_SIZE)
    cpp = PAGE_SIZE // CMP_STRIDE
    perm = jnp.asarray((key % CMP_STRIDE * cpp + key // CMP_STRIDE)[None, :] == key[:, None], BF)
    qspec = pl.BlockSpec((1, ds, Q_LANES), lambda b, g, pt: (b, 0, g))
    bg_spec = lambda r, w: pl.BlockSpec((1, 1, r, w), lambda b, g, pt: (b, g, 0, 0))

    o_c, score = pl.pallas_call(
        functools.partial(_sample_cmp_kernel, past=past, ds=ds, ns=ns),
        grid_spec=pltpu.PrefetchScalarGridSpec(
            num_scalar_prefetch=1, grid=(db, KV_HEADS),
            in_specs=[qspec, pl.BlockSpec(memory_space=pl.ANY), _const_spec(perm.shape),
                      _const_spec(wbig.shape), _const_spec(hb.shape), _const_spec(w2.shape),
                      _const_spec(caux.shape), _const_spec(m_mat.shape)],
            out_specs=[bg_spec(rows, KV_LANES), bg_spec(ds, nbp)],
            scratch_shapes=[pltpu.VMEM((2, n_pages, KV_LANES, PAGE_SIZE), F32),
                            pltpu.VMEM((nc, CMP_STRIDE * KV_LANES), BF),
                            pltpu.SemaphoreType.DMA((2,))]),
        out_shape=[jax.ShapeDtypeStruct((db, KV_HEADS, rows, KV_LANES), F32),
                   jax.ShapeDtypeStruct((db, KV_HEADS, ds, nbp), F32)],
        compiler_params=_cparams(("arbitrary", "arbitrary")),
    )(page_table, q, cache_t, perm, wbig, hb, w2, caux, m_mat)

    nq = db * KV_HEADS * ds
    nq_pad = -(-nq // LANES) * LANES
    score2 = jnp.pad(score.reshape(nq, nbp), ((0, nq_pad - nq), (0, 0)), constant_values=-2.0)
    sel = pl.pallas_call(
        functools.partial(_topk_kernel, ns=ns),
        grid=(nq_pad // LANES,),
        in_specs=[pl.BlockSpec((LANES, nbp), lambda i: (i, 0))],
        out_specs=pl.BlockSpec((LANES, nbp), lambda i: (i, 0)),
        out_shape=jax.ShapeDtypeStruct((nq_pad, nbp), F32),
        compiler_params=_cparams(("arbitrary",)),
    )(score2)[:nq].reshape(db, KV_HEADS, ds, nbp)

    return pl.pallas_call(
        functools.partial(_sample_sel_kernel, past=past, ds=ds, wb=wb, ns=ns),
        grid_spec=pltpu.PrefetchScalarGridSpec(
            num_scalar_prefetch=1, grid=(db, KV_HEADS),
            in_specs=[qspec,
                      pl.BlockSpec((1, ds, 2 * KV_LANES), lambda b, g, pt: (b, 0, g)),
                      pl.BlockSpec((1, ds, KV_LANES), lambda b, g, pt: (b, 0, g)),
                      pl.BlockSpec((1, KV_LANES, wb), lambda b, g, pt: (b, g, 0)),
                      pl.BlockSpec((1, 1, ds, N_GATE), lambda b, g, pt: (g, b, 0, 0)),
                      bg_spec(rows, KV_LANES), bg_spec(ds, nbp),
                      pl.BlockSpec(memory_space=pl.ANY), _const_spec(eaux.shape)],
            out_specs=qspec,
            scratch_shapes=[pltpu.VMEM((2, KV_LANES, past), F32), pltpu.SemaphoreType.DMA((2,))]),
        out_shape=jax.ShapeDtypeStruct(q.shape, BF),
        compiler_params=_cparams(("arbitrary", "arbitrary")),
    )(page_table, q, kvp, kvw, win_t, gates, o_c, sel, cache_t, eaux)


def _layer(x, mod, attn_fn, conv_hist, w, tiles):
    (tb_f, rb_f), (tb_p, rb_p), (tb_c, rb_c, rs_c) = tiles
    h1 = _ffn_call(x, mod, 0, w['g1'], w['ff1_in'], w['ff1_out'], tb_f, rb_f)[0]
    q, kvp, kvpb, kvw, kvwb, gates, u, sga, sgc = _proj_call(h1, mod, w['g2'], w['proj'], tb_p, rb_p)
    conv = _conv_call(u, conv_hist, w['w_dw'], w['b_dw'], w['g_cln'], w['b_cln'], tb_c, rb_c, rs_c)
    attn = attn_fn(q, kvp, kvpb, kvw, kvwb, gates)
    h2 = _merge_call(h1, attn, conv, sga, sgc, mod, w['br_attn'], w['br_conv'], w['out'], tb_f, rb_f)
    _, y = _ffn_call(h2, mod, 6, w['g3'], w['ff2_in'], w['ff2_out'], tb_f, rb_f, gf=w['g_final'])
    return y, kvp, kvw, u


def kernel(x_prompt, x_sample, c_prompt, c_sample, cache_kv, state_win, state_conv, page_table, w_ada, b_ada,
           g_norm1, g_norm2, g_norm3, g_final, w_ff1_in, w_ff1_out, w_ff2_in, w_ff2_out, w_in, pe_cmp, w_phi1,
           b_phi1, w_phi2, w_dw, b_dw, g_conv_ln, b_conv_ln, w_br_attn, w_br_conv, w_out):
    assert w_ada.shape[0] == 1 and cache_kv.shape[2] == 1, "single layer"
    b, t_len, d = x_prompt.shape
    db, ds, _ = x_sample.shape
    wb = state_win.shape[1]
    cw1 = CONV_WIDTH - 1
    q_w = N_HEADS * HEAD_DIM
    kvp_w = KV_HEADS * 4 * HEAD_DIM
    kvw_w = KV_HEADS * 2 * HEAD_DIM
    ng_w = 3 * N_HEADS
    c_dim = w_dw.shape[2]
    assert t_len >= wb and t_len >= cw1

    wi = w_in[0]
    o0, o1, o2, o3, o4 = np.cumsum([q_w, kvp_w, kvw_w, ng_w, 2 * c_dim])
    perm = np.array([r * N_HEADS + g * HPG + h for g in range(KV_HEADS) for r in range(3) for h in range(HPG)])
    w_ng = jnp.pad(wi[:, o2:o3][:, perm], ((0, 0), (0, LANES - ng_w)))
    proj = tuple(v.astype(BF) for v in (wi[:, :o0] * HEAD_DIM ** -0.5, wi[:, o0:o1], wi[:, o1:o2], w_ng,
                                        wi[:, o3:o4], wi[:, o4:]))
    w = dict(g1=g_norm1[0], g2=g_norm2[0], g3=g_norm3[0], g_final=g_final,
             ff1_in=w_ff1_in[0].astype(BF), ff1_out=w_ff1_out[0].astype(BF),
             ff2_in=w_ff2_in[0].astype(BF), ff2_out=w_ff2_out[0].astype(BF), proj=proj,
             w_dw=w_dw[0], b_dw=b_dw[0], g_cln=g_conv_ln[0], b_cln=b_conv_ln[0],
             br_attn=w_br_attn[0].astype(BF), br_conv=w_br_conv[0].astype(BF), out=w_out[0].astype(BF))
    wbig, w2 = _cmp_weights(w_phi1[0], w_phi2[0])
    hb = _cmp_bias_call(pe_cmp[0], wbig, b_phi1[0])

    nseq = b + db
    pad = -nseq % 8
    c_all = jnp.concatenate([c_prompt, c_sample, jnp.zeros((pad, d), F32)], axis=0)
    mod = _mod_call(c_all, w_ada[0], b_ada[0])
    mod_p = mod[:b].reshape(b, 1, N_MOD * d)
    mod_s = mod[b:nseq].reshape(db, 1, N_MOD * d)

    attn_p = lambda q, kvp, kvpb, kvw, kvwb, gates: _nsa_prompt_call(q, kvp, kvpb, kvwb, gates, wbig, hb, w2)
    y_p, kvp_p, kvw_p, u_p = _layer(x_prompt, mod_p, attn_p, None, w,
                                    ((1, 512), (1, 256), (1, 256, 32)))

    cache_t = jnp.transpose(cache_kv, (0, 2, 3, 4, 5, 1)).reshape(cache_kv.shape[0], kvp_w, PAGE_SIZE)
    win_t = jnp.transpose(state_win, (0, 2, 3, 4, 5, 1)).reshape(db, kvw_w, wb)
    attn_s = lambda q, kvp, kvpb, kvw, kvwb, gates: _nsa_sample_call(page_table, q, kvp, kvw, win_t, gates,
                                                                      cache_t, wbig, hb, w2)
    sb = min(db, 64)
    y_s, kvp_s, kvw_s, u_s = _layer(x_sample, mod_s, attn_s, state_conv[:, :, 0], w,
                                    ((sb, ds), (min(db, 32), ds), (min(db, 8), ds, ds)))

    kv5 = lambda a: a.reshape(a.shape[0], a.shape[1], 1, KV_HEADS, 4, HEAD_DIM)
    win5 = lambda a: a.reshape(a.shape[0], a.shape[1], 1, KV_HEADS, 2, HEAD_DIM)
    new_win_p = win5(kvw_p[:, t_len - wb:])
    new_win_s = jnp.concatenate([state_win, win5(kvw_s)], axis=1)[:, -wb:]
    new_conv_p = u_p[:, t_len - cw1:][:, :, None, :]
    new_conv_s = jnp.concatenate([state_conv[:, :, 0], u_s], axis=1)[:, -cw1:][:, :, None, :]
    return (y_p, y_s, kv5(kvp_p), kv5(kvp_s), new_win_p, new_win_s, new_conv_p, new_conv_s)
```

```python
import functools

import numpy as np
import jax
import jax.numpy as jnp
from jax import lax
from jax.experimental import pallas as pl
from jax.experimental.pallas import tpu as pltpu

F32 = jnp.float32
BF = jnp.bfloat16

N_HEADS = 16
HEAD_DIM = 64
KV_HEADS = 4
HPG = N_HEADS // KV_HEADS
CMP_LEN = 32
CMP_STRIDE = 16
SEL_BLOCK = 64
N_SEL = 16
WINDOW = 512
PHI_HID = 128
CONV_WIDTH = 31
N_MOD = 9
EPS = 1e-6
NEG = -1e30
FORCE = 1e4
PAGE_SIZE = 128

LANES = 128
KV_LANES = 2 * HEAD_DIM
Q_LANES = HPG * HEAD_DIM
N_GATE = 3 * HPG
N_POS = 6
VMEM_LIMIT = 56 * 1024 * 1024


def _cparams(sem):
    return pltpu.CompilerParams(dimension_semantics=sem, vmem_limit_bytes=VMEM_LIMIT)


def _const_spec(shape):
    nd = len(shape)
    return pl.BlockSpec(shape, lambda *_: (0,) * nd, pipeline_mode=pl.Buffered(1))


def _tok(ref):
    tb, rb, w = ref.shape
    return ref[0] if tb == 1 else ref[...].reshape(tb * rb, w)


def _modrow(ref, rb):
    tb, _, w = ref.shape
    if tb == 1:
        return ref[0]
    return jnp.broadcast_to(ref[...], (tb, rb, w)).reshape(tb * rb, w)


def _put(ref, val):
    tb, rb, w = ref.shape
    if tb == 1:
        ref[0] = val.astype(ref.dtype)
    else:
        ref[...] = val.reshape(tb, rb, w).astype(ref.dtype)


def _rms(x, g):
    return x * lax.rsqrt(jnp.mean(x * x, axis=-1, keepdims=True) + EPS) * g


def _silu(x):
    return x * jax.nn.sigmoid(x)


def _mm(a, b):
    return jnp.dot(a, b, preferred_element_type=F32)


def _mm_t(a, b):
    return lax.dot_general(a, b, (((1,), (1,)), ((), ())), preferred_element_type=F32)


def _tok_spec(tb, rb, w):
    return pl.BlockSpec((tb, rb, w), lambda i, j: (i, j, 0))


def _mod_spec(tb, d, col):
    return pl.BlockSpec((tb, 1, d), lambda i, j: (i, 0, col))


def _mod_kernel(c_ref, w_ref, b_ref, o_ref):
    a = _silu(c_ref[...]).astype(BF)
    o_ref[...] = _mm(a, w_ref[...].astype(BF)) + b_ref[...]


def _mod_call(c, w, b):
    m, d = c.shape
    n = w.shape[1]
    tn = d
    return pl.pallas_call(
        _mod_kernel,
        grid=(n // tn,),
        in_specs=[pl.BlockSpec((m, d), lambda j: (0, 0)),
                  pl.BlockSpec((d, tn), lambda j: (0, j)),
                  pl.BlockSpec((1, tn), lambda j: (0, j))],
        out_specs=pl.BlockSpec((m, tn), lambda j: (0, j)),
        out_shape=jax.ShapeDtypeStruct((m, n), F32),
        compiler_params=_cparams(("arbitrary",)),
    )(c, w, b.reshape(1, n))


def _ffn_kernel(*refs, d_ff, fc, final):
    if final:
        x_ref, sh_ref, sc_ref, gt_ref, g_ref, wi_ref, wo_ref, gf_ref, h_ref, y_ref = refs
    else:
        x_ref, sh_ref, sc_ref, gt_ref, g_ref, wi_ref, wo_ref, h_ref = refs
    rb = x_ref.shape[1]
    x = _tok(x_ref)
    n = (_rms(x, g_ref[...]) * (1.0 + _modrow(sc_ref, rb)) + _modrow(sh_ref, rb)).astype(BF)
    acc = jnp.zeros(x.shape, F32)
    for c in range(d_ff // fc):
        gate = _mm(n, wi_ref[:, c * fc:(c + 1) * fc])
        up = _mm(n, wi_ref[:, d_ff + c * fc:d_ff + (c + 1) * fc])
        acc = acc + _mm((_silu(gate) * up).astype(BF), wo_ref[c * fc:(c + 1) * fc, :])
    h = x + _modrow(gt_ref, rb) * (0.5 * acc)
    _put(h_ref, h)
    if final:
        _put(y_ref, _rms(h, gf_ref[...]))


def _ffn_call(x, mod, col0, g, wi, wo, tb, rb, gf=None):
    nb, r, d = x.shape
    d_ff = wo.shape[0]
    fc = d_ff // 2
    final = gf is not None
    ins = [x, mod, mod, mod, g.reshape(1, d), wi, wo]
    specs = [_tok_spec(tb, rb, d), _mod_spec(tb, d, col0), _mod_spec(tb, d, col0 + 1),
             _mod_spec(tb, d, col0 + 2), _const_spec((1, d)), _const_spec(wi.shape), _const_spec(wo.shape)]
    out_shape = [jax.ShapeDtypeStruct(x.shape, F32)]
    out_specs = [_tok_spec(tb, rb, d)]
    if final:
        ins.append(gf.reshape(1, d))
        specs.append(_const_spec((1, d)))
        out_shape.append(jax.ShapeDtypeStruct(x.shape, F32))
        out_specs.append(_tok_spec(tb, rb, d))
    return pl.pallas_call(
        functools.partial(_ffn_kernel, d_ff=d_ff, fc=fc, final=final),
        grid=(nb // tb, r // rb),
        in_specs=specs, out_specs=out_specs, out_shape=out_shape,
        compiler_params=_cparams(("arbitrary", "arbitrary")),
    )(*ins)


def _proj_kernel(h_ref, sh_ref, sc_ref, g_ref, wq_ref, wkvp_ref, wkvw_ref, wng_ref, wglu_ref, wmg_ref,
                 q_ref, kvp_ref, kvpb_ref, kvw_ref, kvwb_ref, gates_ref, u_ref, sga_ref, sgc_ref):
    tb, rb, d = h_ref.shape
    n = (_rms(_tok(h_ref), g_ref[...]) * (1.0 + _modrow(sc_ref, rb)) + _modrow(sh_ref, rb)).astype(BF)
    _put(q_ref, _mm(n, wq_ref[...]))
    kvp = _mm(n, wkvp_ref[...])
    _put(kvp_ref, kvp)
    _put(kvpb_ref, kvp)
    kvw = _mm(n, wkvw_ref[...])
    _put(kvw_ref, kvw)
    _put(kvwb_ref, kvw)
    ng = jax.nn.sigmoid(_mm(n, wng_ref[...]))
    for g in range(KV_HEADS):
        gates_ref[g] = ng[:, g * N_GATE:(g + 1) * N_GATE].reshape(tb, rb, N_GATE)
    glu = _mm(n, wglu_ref[...])
    c = glu.shape[1] // 2
    _put(u_ref, glu[:, :c] * jax.nn.sigmoid(glu[:, c:]))
    mg = jax.nn.sigmoid(_mm(n, wmg_ref[...]))
    _put(sga_ref, mg[:, :d])
    _put(sgc_ref, mg[:, d:])


def _proj_call(h, mod, g, ws, tb, rb):
    nb, r, d = h.shape
    wq, wkvp, wkvw, wng, wglu, wmg = ws
    c = wglu.shape[1] // 2
    widths = [(wq.shape[1], BF), (wkvp.shape[1], F32), (wkvp.shape[1], BF), (wkvw.shape[1], F32),
              (wkvw.shape[1], BF)]
    out_shape = [jax.ShapeDtypeStruct((nb, r, w), dt) for w, dt in widths]
    out_specs = [_tok_spec(tb, rb, w) for w, _ in widths]
    out_shape.append(jax.ShapeDtypeStruct((KV_HEADS, nb, r, N_GATE), F32))
    out_specs.append(pl.BlockSpec((KV_HEADS, tb, rb, N_GATE), lambda i, j: (0, i, j, 0)))
    for w in (c, d, d):
        out_shape.append(jax.ShapeDtypeStruct((nb, r, w), F32))
        out_specs.append(_tok_spec(tb, rb, w))
    return pl.pallas_call(
        _proj_kernel,
        grid=(nb // tb, r // rb),
        in_specs=[_tok_spec(tb, rb, d), _mod_spec(tb, d, 3), _mod_spec(tb, d, 4), _const_spec((1, d))]
                 + [_const_spec(w.shape) for w in ws],
        out_specs=out_specs, out_shape=out_shape,
        compiler_params=_cparams(("arbitrary", "arbitrary")),
    )(h, mod, mod, g.reshape(1, d), *ws)


CONV_HIST = 32


def _conv_kernel(u_ref, hist_ref, w_ref, b_ref, g_ref, bl_ref, o_ref, sh_ref, *, zero_first, rs):
    tb, rb, c = u_ref.shape
    hr = hist_ref.shape[1]
    seg = CONV_HIST + rb
    hist = hist_ref[...]
    if zero_first:
        hist = jnp.where(pl.program_id(1) > 0, hist, 0.0)
    for s in range(tb):
        if hr < CONV_HIST:
            sh_ref[0, s * seg:s * seg + CONV_HIST - hr, :] = jnp.zeros((CONV_HIST - hr, c), F32)
        sh_ref[0, s * seg + CONV_HIST - hr:s * seg + CONV_HIST, :] = hist[s]
        sh_ref[0, s * seg + CONV_HIST:(s + 1) * seg, :] = u_ref[s]
    ext = sh_ref[0]
    for b in range(1, 8):
        sh_ref[b] = pltpu.roll(ext, tb * seg - b, 0)
    first = CONV_HIST - (CONV_WIDTH - 1)
    w = w_ref[...]
    for s in range(tb):
        for r0 in range(0, rb, rs):
            acc = jnp.broadcast_to(b_ref[...], (rs, c))
            for k in range(CONV_WIDTH):
                a, b = divmod(k + first, 8)
                row = s * seg + r0 + 8 * a
                acc = acc + w[k:k + 1, :] * sh_ref[b, row:row + rs, :]
            xc = acc - jnp.mean(acc, axis=-1, keepdims=True)
            y = xc * lax.rsqrt(jnp.mean(xc * xc, axis=-1, keepdims=True) + EPS) * g_ref[...] + bl_ref[...]
            o_ref[s, r0:r0 + rs, :] = _silu(y).astype(o_ref.dtype)


def _conv_call(u, hist, w, b, g, bl, tb, rb, rs):
    nb, r, c = u.shape
    row = lambda v: v.reshape(1, c)
    if hist is None:
        hist_arr, hr = u, CONV_HIST
        hspec = pl.BlockSpec((tb, hr, c), lambda i, j: (i, jnp.maximum(j * (rb // CONV_HIST) - 1, 0), 0))
    else:
        hist_arr, hr = hist, hist.shape[1]
        hspec = pl.BlockSpec((tb, hr, c), lambda i, j: (i, 0, 0))
    return pl.pallas_call(
        functools.partial(_conv_kernel, zero_first=hist is None, rs=rs),
        grid=(nb // tb, r // rb),
        in_specs=[_tok_spec(tb, rb, c), hspec, _const_spec(w.shape)] + [_const_spec((1, c))] * 3,
        out_specs=_tok_spec(tb, rb, c),
        out_shape=jax.ShapeDtypeStruct(u.shape, BF),
        scratch_shapes=[pltpu.VMEM((8, tb * (CONV_HIST + rb), c), F32)],
        compiler_params=_cparams(("arbitrary", "arbitrary")),
    )(u, hist_arr, w, row(b), row(g), row(bl))


def _merge_kernel(h_ref, attn_ref, conv_ref, sga_ref, sgc_ref, gt_ref, wa_ref, wc_ref, wo_ref, o_ref):
    rb = h_ref.shape[1]
    m = _tok(sga_ref) * _mm(_tok(attn_ref), wa_ref[...]) + _tok(sgc_ref) * _mm(_tok(conv_ref), wc_ref[...])
    _put(o_ref, _tok(h_ref) + _modrow(gt_ref, rb) * _mm(m.astype(BF), wo_ref[...]))


def _merge_call(h, attn, conv, sga, sgc, mod, wa, wc, wo, tb, rb):
    nb, r, d = h.shape
    return pl.pallas_call(
        _merge_kernel,
        grid=(nb // tb, r // rb),
        in_specs=[_tok_spec(tb, rb, d), _tok_spec(tb, rb, attn.shape[2]), _tok_spec(tb, rb, conv.shape[2]),
                  _tok_spec(tb, rb, d), _tok_spec(tb, rb, d), _mod_spec(tb, d, 5),
                  _const_spec(wa.shape), _const_spec(wc.shape), _const_spec(wo.shape)],
        out_specs=_tok_spec(tb, rb, d),
        out_shape=jax.ShapeDtypeStruct(h.shape, F32),
        compiler_params=_cparams(("arbitrary", "arbitrary")),
    )(h, attn, conv, sga, sgc, mod, wa, wc, wo)


def _cmp_weights(w_phi1, w_phi2):
    c = CMP_LEN // CMP_STRIDE
    w = w_phi1.reshape(2, c, CMP_STRIDE, HEAD_DIM, PHI_HID)
    eye = jnp.eye(2, dtype=w.dtype)
    wbig = jnp.einsum('kosdh,kc->skdcoh', w, eye).reshape(CMP_STRIDE * KV_LANES, 2 * c * PHI_HID)
    w2 = jnp.einsum('khd,kc->khcd', w_phi2, eye).reshape(2 * PHI_HID, KV_LANES)
    return wbig.astype(BF), w2.astype(BF)


def _cmp_bias_kernel(pe_ref, wbig_ref, b_ref, o_ref):
    res = _mm(pe_ref[...].astype(BF), wbig_ref[...])
    hs = [b_ref[:, k * PHI_HID:(k + 1) * PHI_HID]
          + res[0:1, 2 * k * PHI_HID:(2 * k + 1) * PHI_HID]
          + res[1:2, (2 * k + 1) * PHI_HID:(2 * k + 2) * PHI_HID] for k in range(2)]
    o_ref[...] = jnp.concatenate(hs, axis=1)


def _cmp_bias_call(pe_cmp, wbig, b_phi1):
    c = CMP_LEN // CMP_STRIDE
    pe = pe_cmp.reshape(2, c, CMP_STRIDE, HEAD_DIM).transpose(1, 2, 0, 3).reshape(c, CMP_STRIDE * KV_LANES)
    pe = jnp.concatenate([pe, jnp.zeros((8 - c, pe.shape[1]), pe.dtype)], axis=0)
    return pl.pallas_call(
        _cmp_bias_kernel,
        out_shape=jax.ShapeDtypeStruct((1, 2 * PHI_HID), F32),
        compiler_params=pltpu.CompilerParams(vmem_limit_bytes=VMEM_LIMIT),
    )(pe, wbig, b_phi1.reshape(1, 2 * PHI_HID))


def _compress(lhs, wbig_ref, hb_ref, w2_ref):
    part = _mm(lhs, wbig_ref[...])
    n = part.shape[0]
    nxt = pltpu.roll(part, n - 1, 0)
    hb = hb_ref[...]
    hs = [hb[:, k * PHI_HID:(k + 1) * PHI_HID]
          + part[:, 2 * k * PHI_HID:(2 * k + 1) * PHI_HID]
          + nxt[:, (2 * k + 1) * PHI_HID:(2 * k + 2) * PHI_HID] for k in range(2)]
    h = jnp.concatenate(hs, axis=1)
    return _mm(_silu(h).astype(BF), w2_ref[...])


def _stack_q(q, r):
    q = q.astype(F32)
    return jnp.concatenate([q[:, h * HEAD_DIM:(h + 1) * HEAD_DIM] for h in range(HPG)], axis=0)


def _row_info(r, g, t0):
    row = lax.broadcasted_iota(jnp.int32, (HPG * r, 1), 0)
    t = t0 + (row & (r - 1))
    head = g * HPG + lax.shift_right_logical(row, r.bit_length() - 1)
    slope = jnp.exp2(-(8.0 / N_HEADS) * (head + 1).astype(F32))
    return t, slope


def _q_aux(slope, width, pos0, selneg=None):
    hi = slope.astype(BF).astype(F32)
    rest = slope - hi
    mid = rest.astype(BF).astype(F32)
    lo = rest - mid
    lane = lax.broadcasted_iota(jnp.int32, (slope.shape[0], width), 1) - pos0
    base = jnp.zeros((slope.shape[0], width), F32) if selneg is None else selneg
    aux = jnp.where((lane == 0) | (lane == 3), hi, base)
    aux = jnp.where((lane == 1) | (lane == 4), mid, aux)
    aux = jnp.where((lane == 2) | (lane == 5), lo, aux)
    return aux.astype(BF)


def _k_aux(pos, blocks, width, pos0):
    pos = np.asarray(pos)
    a = np.zeros((pos.shape[0], width), np.float32)
    if blocks is not None:
        a[np.arange(pos.shape[0]), np.asarray(blocks)] = 1.0
    a[:, pos0:pos0 + 3] = (pos // SEL_BLOCK * SEL_BLOCK)[:, None]
    a[:, pos0 + 3:pos0 + 6] = (pos % SEL_BLOCK)[:, None]
    return a


def _topk_mask(xt, k):
    nb = xt.shape[0]
    rowf = lax.broadcasted_iota(jnp.int32, xt.shape, 0).astype(F32)

    def body(_, carry):
        x, sel = carry
        m = jnp.max(x, axis=0, keepdims=True)
        idx = jnp.min(jnp.where(x == m, rowf, float(nb)), axis=0, keepdims=True)
        pick = rowf == idx
        return jnp.where(pick, -3.0, x), jnp.where(pick, 1.0, sel)

    _, sel = lax.fori_loop(0, k, body, (xt, jnp.zeros_like(xt)))
    return sel


def _select(score, ns):
    nbp = score.shape[1]
    nb8 = -(-ns // 8) * 8
    sel_t = _topk_mask(score.T[0:nb8], N_SEL)
    if nb8 < nbp:
        sel_t = jnp.concatenate([sel_t, jnp.zeros((nbp - nb8, sel_t.shape[1]), F32)], axis=0)
    return sel_t.T


def _cmp_branch(qx, kvc, caux_ref, t, r, n_cmp, ns, m_ref):
    nc = kvc.shape[0]
    nbp = m_ref.shape[1]
    ci = lax.broadcasted_iota(jnp.int32, (1, nc), 1)
    mask = (t >= ci * CMP_STRIDE + (CMP_LEN - 1)) & (ci < n_cmp)
    s = jnp.where(mask, _mm_t(qx, jnp.concatenate([kvc, caux_ref[...]], axis=1)), NEG)
    e = jnp.where(mask, jnp.exp(s - jnp.max(s, axis=-1, keepdims=True)), 0.0)
    p = e / jnp.maximum(jnp.sum(e, axis=-1, keepdims=True), 1e-30)
    o_c = _mm(p.astype(BF), kvc)
    psum = p[0:r] + p[r:2 * r] + p[2 * r:3 * r] + p[3 * r:4 * r]
    hi = psum.astype(BF).astype(F32)
    mid = (psum - hi).astype(BF).astype(F32)
    lo = psum - hi - mid
    res = _mm(jnp.concatenate([hi, mid, lo, jnp.zeros_like(lo)], axis=0).astype(BF), m_ref[...])
    imp = res[0:r] + res[r:2 * r] + res[2 * r:3 * r]
    blk = lax.broadcasted_iota(jnp.int32, (1, nbp), 1)
    cur = lax.shift_right_logical(t[0:r], SEL_BLOCK.bit_length() - 1)
    forced = (blk == 0) | (blk == cur) | (blk == cur - 1)
    score = jnp.where(blk <= cur, imp + jnp.where(forced, FORCE, 0.0), -1.0)
    return o_c, jnp.where(blk < ns, score, -2.0)


def _ones_v(kv):
    return jnp.where(lax.broadcasted_iota(jnp.int32, kv.shape, 1) < HEAD_DIM, jnp.ones_like(kv), kv)


def _gate_col(gt, branch):
    return jnp.concatenate([gt[:, branch * HPG + h:branch * HPG + h + 1] for h in range(HPG)], axis=0)


def _unstack(o, r):
    return jnp.concatenate([o[h * r:(h + 1) * r] for h in range(HPG)], axis=1)


def _imp_matrix(nc, n_cmp, nbp, ns):
    i = np.arange(nc)[:, None]
    j = np.arange(nbp)[None, :]
    r = SEL_BLOCK // CMP_STRIDE
    c = CMP_LEN // CMP_STRIDE
    return jnp.asarray((i >= r * j - (c - 1)) & (i <= r * j + r - 1) & (i < n_cmp) & (j < ns), BF)


def _nsa_prompt_kernel(q_ref, kc_ref, ks_ref, kw_ref, gt_ref, wbig_ref, hb_ref, w2_ref, kaux_ref, caux_ref,
                       m_ref, o_ref, kvc_ref, *, tq, t_len, ck, ns):
    g = pl.program_id(1)
    i = pl.program_id(2)
    n = t_len // CMP_STRIDE
    rows = HPG * tq

    @pl.when(i == 0)
    def _():
        lhs = jnp.concatenate([kc_ref[0, pl.ds(s, n, stride=CMP_STRIDE), :].astype(BF)
                               for s in range(CMP_STRIDE)], axis=1)
        kvc = _compress(lhs, wbig_ref, hb_ref, w2_ref)
        kvc_ref[...] = kvc.astype(BF)

    start = i * tq
    t, slope = _row_info(tq, g, start)
    qs = _stack_q(q_ref[0], tq)
    qk = jnp.concatenate([qs, jnp.zeros_like(qs)], axis=1).astype(BF)
    qx = jnp.concatenate([qk, _q_aux(slope, LANES, ns)], axis=1)

    o_c, score = _cmp_branch(qx, kvc_ref[...], caux_ref, t, tq, n - 1, ns, m_ref)
    sel = jnp.concatenate([_select(score[j:j + LANES], ns) for j in range(0, tq, LANES)], axis=0)

    lo = pl.multiple_of(jnp.maximum(start - WINDOW, 0), tq)
    kw = kw_ref[0, pl.ds(lo, WINDOW + tq), :]
    s = _mm_t(qx, jnp.concatenate([kw, kaux_ref[pl.ds(lo, WINDOW + tq), :]], axis=1))
    dist = t - (lo + lax.broadcasted_iota(jnp.int32, (1, WINDOW + tq), 1))
    s = jnp.where((dist >= 0) & (dist <= WINDOW), s, NEG)
    e = jnp.exp(s - jnp.max(s, axis=-1, keepdims=True)).astype(BF)
    o_w = _mm(e, _ones_v(kw))
    o_w = o_w / o_w[:, 0:1]

    selneg = jnp.concatenate([(sel - 1.0) * (-NEG)] * HPG, axis=0)
    qx = jnp.concatenate([qk, _q_aux(slope, LANES, ns, selneg)], axis=1)

    def chunk(c, carry, causal):
        m, acc = carry
        k0 = pl.multiple_of(c * ck, ck)
        kv = ks_ref[0, pl.ds(k0, ck), :]
        s = _mm_t(qx, jnp.concatenate([kv, kaux_ref[pl.ds(k0, ck), :]], axis=1))
        if causal:
            s = jnp.where(t >= k0 + lax.broadcasted_iota(jnp.int32, (1, ck), 1), s, NEG)
        m_new = jnp.maximum(m, jnp.max(s, axis=-1, keepdims=True))
        e = jnp.exp(s - m_new).astype(BF)
        acc = jnp.exp(m - m_new) * acc + _mm(e, _ones_v(kv))
        return m_new, acc

    init = (jnp.full((rows, 1), NEG, F32), jnp.zeros((rows, KV_LANES), F32))
    last = (start + tq - 1) // ck
    carry = lax.fori_loop(0, last, lambda c, cr: chunk(c, cr, False), init)
    _, acc = chunk(last, carry, True)
    o_s = acc / acc[:, 0:1]

    gt = gt_ref[0, 0]
    o = _gate_col(gt, 0) * o_c + _gate_col(gt, 1) * o_s + _gate_col(gt, 2) * o_w
    o_ref[0] = _unstack(o[:, HEAD_DIM:], tq).astype(o_ref.dtype)


def _nsa_prompt_call(q, kvp, kvpb, kvwb, gates, wbig, hb, w2, tq=128, ck=512):
    b, t_len, _ = q.shape
    nc = t_len // CMP_STRIDE
    ns = t_len // SEL_BLOCK
    assert t_len % ck == 0 and t_len >= WINDOW + tq and N_SEL <= ns <= LANES - N_POS and tq % LANES == 0
    pos = np.arange(t_len)
    kaux = jnp.asarray(_k_aux(pos, pos // SEL_BLOCK, LANES, ns), BF)
    caux = jnp.asarray(_k_aux(np.arange(nc) * CMP_STRIDE + CMP_LEN - 1, None, LANES, ns), BF)
    m_mat = _imp_matrix(nc, nc - 1, LANES, ns)
    return pl.pallas_call(
        functools.partial(_nsa_prompt_kernel, tq=tq, t_len=t_len, ck=ck, ns=ns),
        grid=(b, KV_HEADS, t_len // tq),
        in_specs=[pl.BlockSpec((1, tq, Q_LANES), lambda b, g, i: (b, i, g)),
                  pl.BlockSpec((1, t_len, KV_LANES), lambda b, g, i: (b, 0, 2 * g)),
                  pl.BlockSpec((1, t_len, KV_LANES), lambda b, g, i: (b, 0, 2 * g + 1)),
                  pl.BlockSpec((1, t_len, KV_LANES), lambda b, g, i: (b, 0, g)),
                  pl.BlockSpec((1, 1, tq, N_GATE), lambda b, g, i: (g, b, i, 0)),
                  _const_spec(wbig.shape), _const_spec(hb.shape), _const_spec(w2.shape),
                  _const_spec(kaux.shape), _const_spec(caux.shape), _const_spec(m_mat.shape)],
        out_specs=pl.BlockSpec((1, tq, Q_LANES), lambda b, g, i: (b, i, g)),
        out_shape=jax.ShapeDtypeStruct(q.shape, BF),
        scratch_shapes=[pltpu.VMEM((nc, KV_LANES), BF)],
        compiler_params=_cparams(("arbitrary", "arbitrary", "arbitrary")),
    )(q, kvp, kvpb, kvwb, gates, wbig, hb, w2, kaux, caux, m_mat)


NEW_PAD = 128


def _page_stream(pt_ref, cache_ref, sem, row0, n_pages, dst_of):
    b = pl.program_id(0)
    g = pl.program_id(1)
    step = b * KV_HEADS + g
    total = pl.num_programs(0) * KV_HEADS
    slot = step % 2

    def copy(bb, gg, p, sl):
        rows = pl.ds(pl.multiple_of(gg * 2 * KV_LANES + row0, KV_LANES), KV_LANES)
        return pltpu.make_async_copy(cache_ref.at[pt_ref[bb * n_pages + p], rows, :], dst_of(sl, p), sem.at[sl])

    def fetch(st, sl):
        bb = st // KV_HEADS
        gg = st % KV_HEADS

        def body(p, carry):
            copy(bb, gg, p, sl).start()
            return carry

        lax.fori_loop(0, n_pages, body, 0, unroll=4)

    @pl.when(step == 0)
    def _():
        fetch(step, slot)

    @pl.when(step + 1 < total)
    def _():
        fetch(step + 1, 1 - slot)

    def wait_body(p, carry):
        copy(b, g, p, slot).wait()
        return carry

    lax.fori_loop(0, n_pages, wait_body, 0, unroll=4)
    return slot


def _sample_cmp_kernel(pt_ref, q_ref, cache_ref, perm_ref, wbig_ref, hb_ref, w2_ref, caux_ref, m_ref,
                       oc_ref, score_ref, buf_ref, lhs_ref, sem, *, past, ds, ns):
    g = pl.program_id(1)
    n_pages = past // PAGE_SIZE
    slot = _page_stream(pt_ref, cache_ref, sem, 0, n_pages, lambda sl, p: buf_ref.at[sl, p])

    cpp = PAGE_SIZE // CMP_STRIDE
    perm = perm_ref[...]

    def regroup(i, carry):
        res = [_mm_t(perm, buf_ref[slot, 2 * i + j].astype(BF)) for j in range(2)]
        r0 = pl.multiple_of(i * 2 * cpp, 2 * cpp)
        for s in range(CMP_STRIDE):
            rows = jnp.concatenate([r[s * cpp:(s + 1) * cpp] for r in res], axis=0)
            lhs_ref[pl.ds(r0, 2 * cpp), s * KV_LANES:(s + 1) * KV_LANES] = rows.astype(BF)
        return carry

    lax.fori_loop(0, n_pages // 2, regroup, 0, unroll=True)
    n = past // CMP_STRIDE
    kvc = _compress(lhs_ref[...], wbig_ref, hb_ref, w2_ref).astype(BF)

    t, slope = _row_info(ds, g, past)
    qs = _stack_q(q_ref[0], ds)
    qx = jnp.concatenate([qs, jnp.zeros_like(qs)], axis=1).astype(BF)
    qx = jnp.concatenate([qx, _q_aux(slope, LANES, 0)], axis=1)
    o_c, score = _cmp_branch(qx, kvc, caux_ref, t, ds, n - 1, ns, m_ref)
    oc_ref[0, 0] = o_c
    score_ref[0, 0] = score


def _topk_kernel(score_ref, sel_ref, *, ns):
    sel_ref[...] = _select(score_ref[...], ns)


def _sample_sel_kernel(pt_ref, q_ref, kvpn_ref, kvwn_ref, win_ref, gt_ref, oc_ref, sel_ref, cache_ref, eaux_ref,
                       o_ref, buf_ref, sem, *, past, ds, wb, ns):
    g = pl.program_id(1)
    n_pages = past // PAGE_SIZE
    slot = _page_stream(pt_ref, cache_ref, sem, KV_LANES, n_pages,
                        lambda sl, p: buf_ref.at[sl, :, pl.ds(pl.multiple_of(p * PAGE_SIZE, PAGE_SIZE), PAGE_SIZE)])

    t, slope = _row_info(ds, g, past)
    qs = _stack_q(q_ref[0], ds).astype(BF)
    zpad = jnp.zeros((NEW_PAD - ds, KV_LANES), F32)
    lane = lax.broadcasted_iota(jnp.int32, (1, NEW_PAD), 1)

    def new_t(rows):
        x = jnp.concatenate([rows, zpad], axis=0).T.astype(BF)
        return x[0:HEAD_DIM], x[HEAD_DIM:KV_LANES]

    naux = eaux_ref.shape[0]
    selneg = jnp.concatenate([(sel_ref[0, 0][:, 0:naux] - 1.0) * (-NEG)] * HPG, axis=0)
    qa = _q_aux(slope, naux, ns, selneg)
    kn, vn = new_t(kvpn_ref[0][:, KV_LANES:])
    kt = buf_ref[slot, 0:HEAD_DIM, :].astype(BF)
    vt = buf_ref[slot, HEAD_DIM:KV_LANES, :].astype(BF)
    s_p = _mm(qs, kt) + _mm(qa, eaux_ref[:, 0:past])
    s_n = _mm(qs, kn) + _mm(qa, eaux_ref[:, past:past + NEW_PAD])
    s_n = jnp.where(t >= past + lane, s_n, NEG)
    m = jnp.maximum(jnp.max(s_p, axis=-1, keepdims=True), jnp.max(s_n, axis=-1, keepdims=True))
    e_p = jnp.exp(s_p - m)
    e_n = jnp.exp(s_n - m)
    l = jnp.sum(e_p, axis=-1, keepdims=True) + jnp.sum(e_n, axis=-1, keepdims=True)
    o_s = (_mm_t(e_p.astype(BF), vt) + _mm_t(e_n.astype(BF), vn)) / l

    kn, vn = new_t(kvwn_ref[0])
    kt = win_ref[0, 0:HEAD_DIM, :].astype(BF)
    vt = win_ref[0, HEAD_DIM:KV_LANES, :].astype(BF)

    def band(s, kpos):
        dist = t - kpos
        return jnp.where((dist >= 0) & (dist <= WINDOW) & (kpos >= 0), s - slope * dist.astype(F32), NEG)

    s_p = band(_mm(qs, kt), (past - wb) + lax.broadcasted_iota(jnp.int32, (1, wb), 1))
    s_n = band(_mm(qs, kn), past + lane)
    m = jnp.maximum(jnp.max(s_p, axis=-1, keepdims=True), jnp.max(s_n, axis=-1, keepdims=True))
    e_p = jnp.exp(s_p - m)
    e_n = jnp.exp(s_n - m)
    l = jnp.sum(e_p, axis=-1, keepdims=True) + jnp.sum(e_n, axis=-1, keepdims=True)
    o_w = (_mm_t(e_p.astype(BF), vt) + _mm_t(e_n.astype(BF), vn)) / l

    gt = gt_ref[0, 0]
    o = _gate_col(gt, 0) * oc_ref[0, 0][:, HEAD_DIM:] + _gate_col(gt, 1) * o_s + _gate_col(gt, 2) * o_w
    o_ref[0] = _unstack(o, ds).astype(o_ref.dtype)


def _nsa_sample_call(page_table, q, kvp, kvw, win_t, gates, cache_t, wbig, hb, w2):
    db, ds, _ = q.shape
    n_pages = page_table.shape[1]
    past = n_pages * PAGE_SIZE
    wb = win_t.shape[2]
    assert ds < CMP_STRIDE and ds & (ds - 1) == 0 and ds % 8 == 0
    nc = past // CMP_STRIDE
    ns = -(-(past + ds) // SEL_BLOCK)
    nbp = -(-(ns + N_POS) // LANES) * LANES
    rows = HPG * ds
    caux = jnp.asarray(_k_aux(np.arange(nc) * CMP_STRIDE + CMP_LEN - 1, None, LANES, 0), BF)
    m_mat = _imp_matrix(nc, nc - 1, nbp, ns)
    pos = np.arange(past + NEW_PAD)
    naux = -(-(ns + N_POS) // 16) * 16
    eaux = jnp.asarray(_k_aux(pos, np.minimum(pos // SEL_BLOCK, ns), naux, ns).T, BF)
    key = np.arange(PAGE_SIZE)
    cpp = PAGE_SIZE // CMP_STRIDE
    perm = jnp.asarray((key % CMP_STRIDE * cpp + key // CMP_STRIDE)[None, :] == key[:, None], BF)
    qspec = pl.BlockSpec((1, ds, Q_LANES), lambda b, g, pt: (b, 0, g))
    bg_spec = lambda r, w: pl.BlockSpec((1, 1, r, w), lambda b, g, pt: (b, g, 0, 0))

    o_c, score = pl.pallas_call(
        functools.partial(_sample_cmp_kernel, past=past, ds=ds, ns=ns),
        grid_spec=pltpu.PrefetchScalarGridSpec(
            num_scalar_prefetch=1, grid=(db, KV_HEADS),
            in_specs=[qspec, pl.BlockSpec(memory_space=pl.ANY), _const_spec(perm.shape),
                      _const_spec(wbig.shape), _const_spec(hb.shape), _const_spec(w2.shape),
                      _const_spec(caux.shape), _const_spec(m_mat.shape)],
            out_specs=[bg_spec(rows, KV_LANES), bg_spec(ds, nbp)],
            scratch_shapes=[pltpu.VMEM((2, n_pages, KV_LANES, PAGE_SIZE), F32),
                            pltpu.VMEM((nc, CMP_STRIDE * KV_LANES), BF),
                            pltpu.SemaphoreType.DMA((2,))]),
        out_shape=[jax.ShapeDtypeStruct((db, KV_HEADS, rows, KV_LANES), F32),
                   jax.ShapeDtypeStruct((db, KV_HEADS, ds, nbp), F32)],
        compiler_params=_cparams(("arbitrary", "arbitrary")),
    )(page_table.reshape(-1), q, cache_t, perm, wbig, hb, w2, caux, m_mat)

    nq = db * KV_HEADS * ds
    nq_pad = -(-nq // LANES) * LANES
    score2 = jnp.pad(score.reshape(nq, nbp), ((0, nq_pad - nq), (0, 0)), constant_values=-2.0)
    sel = pl.pallas_call(
        functools.partial(_topk_kernel, ns=ns),
        grid=(nq_pad // LANES,),
        in_specs=[pl.BlockSpec((LANES, nbp), lambda i: (i, 0))],
        out_specs=pl.BlockSpec((LANES, nbp), lambda i: (i, 0)),
        out_shape=jax.ShapeDtypeStruct((nq_pad, nbp), F32),
        compiler_params=_cparams(("arbitrary",)),
    )(score2)[:nq].reshape(db, KV_HEADS, ds, nbp)

    return pl.pallas_call(
        functools.partial(_sample_sel_kernel, past=past, ds=ds, wb=wb, ns=ns),
        grid_spec=pltpu.PrefetchScalarGridSpec(
            num_scalar_prefetch=1, grid=(db, KV_HEADS),
            in_specs=[qspec,
                      pl.BlockSpec((1, ds, 2 * KV_LANES), lambda b, g, pt: (b, 0, g)),
                      pl.BlockSpec((1, ds, KV_LANES), lambda b, g, pt: (b, 0, g)),
                      pl.BlockSpec((1, KV_LANES, wb), lambda b, g, pt: (b, g, 0)),
                      pl.BlockSpec((1, 1, ds, N_GATE), lambda b, g, pt: (g, b, 0, 0)),
                      bg_spec(rows, KV_LANES), bg_spec(ds, nbp),
                      pl.BlockSpec(memory_space=pl.ANY), _const_spec(eaux.shape)],
            out_specs=qspec,
            scratch_shapes=[pltpu.VMEM((2, KV_LANES, past), F32), pltpu.SemaphoreType.DMA((2,))]),
        out_shape=jax.ShapeDtypeStruct(q.shape, BF),
        compiler_params=_cparams(("arbitrary", "arbitrary")),
    )(page_table.reshape(-1), q, kvp, kvw, win_t, gates, o_c, sel, cache_t, eaux)


def _layer(x, mod, attn_fn, conv_hist, w, tiles):
    (tb_f, rb_f), (tb_p, rb_p), (tb_c, rb_c, rs_c) = tiles
    h1 = _ffn_call(x, mod, 0, w['g1'], w['ff1_in'], w['ff1_out'], tb_f, rb_f)[0]
    q, kvp, kvpb, kvw, kvwb, gates, u, sga, sgc = _proj_call(h1, mod, w['g2'], w['proj'], tb_p, rb_p)
    conv = _conv_call(u, conv_hist, w['w_dw'], w['b_dw'], w['g_cln'], w['b_cln'], tb_c, rb_c, rs_c)
    attn = attn_fn(q, kvp, kvpb, kvw, kvwb, gates)
    h2 = _merge_call(h1, attn, conv, sga, sgc, mod, w['br_attn'], w['br_conv'], w['out'], tb_f, rb_f)
    _, y = _ffn_call(h2, mod, 6, w['g3'], w['ff2_in'], w['ff2_out'], tb_f, rb_f, gf=w['g_final'])
    return y, kvp, kvw, u


def kernel(x_prompt, x_sample, c_prompt, c_sample, cache_kv, state_win, state_conv, page_table, w_ada, b_ada,
           g_norm1, g_norm2, g_norm3, g_final, w_ff1_in, w_ff1_out, w_ff2_in, w_ff2_out, w_in, pe_cmp, w_phi1,
           b_phi1, w_phi2, w_dw, b_dw, g_conv_ln, b_conv_ln, w_br_attn, w_br_conv, w_out):
    assert w_ada.shape[0] == 1 and cache_kv.shape[2] == 1, "single layer"
    b, t_len, d = x_prompt.shape
    db, ds, _ = x_sample.shape
    wb = state_win.shape[1]
    cw1 = CONV_WIDTH - 1
    q_w = N_HEADS * HEAD_DIM
    kvp_w = KV_HEADS * 4 * HEAD_DIM
    kvw_w = KV_HEADS * 2 * HEAD_DIM
    ng_w = 3 * N_HEADS
    c_dim = w_dw.shape[2]
    assert t_len >= wb and t_len >= cw1

    wi = w_in[0]
    o0, o1, o2, o3, o4 = np.cumsum([q_w, kvp_w, kvw_w, ng_w, 2 * c_dim])
    perm = np.array([r * N_HEADS + g * HPG + h for g in range(KV_HEADS) for r in range(3) for h in range(HPG)])
    w_ng = jnp.pad(wi[:, o2:o3][:, perm], ((0, 0), (0, LANES - ng_w)))
    proj = tuple(v.astype(BF) for v in (wi[:, :o0] * HEAD_DIM ** -0.5, wi[:, o0:o1], wi[:, o1:o2], w_ng,
                                        wi[:, o3:o4], wi[:, o4:]))
    w = dict(g1=g_norm1[0], g2=g_norm2[0], g3=g_norm3[0], g_final=g_final,
             ff1_in=w_ff1_in[0].astype(BF), ff1_out=w_ff1_out[0].astype(BF),
             ff2_in=w_ff2_in[0].astype(BF), ff2_out=w_ff2_out[0].astype(BF), proj=proj,
             w_dw=w_dw[0], b_dw=b_dw[0], g_cln=g_conv_ln[0], b_cln=b_conv_ln[0],
             br_attn=w_br_attn[0].astype(BF), br_conv=w_br_conv[0].astype(BF), out=w_out[0].astype(BF))
    wbig, w2 = _cmp_weights(w_phi1[0], w_phi2[0])
    hb = _cmp_bias_call(pe_cmp[0], wbig, b_phi1[0])

    nseq = b + db
    pad = -nseq % 8
    c_all = jnp.concatenate([c_prompt, c_sample, jnp.zeros((pad, d), F32)], axis=0)
    mod = _mod_call(c_all, w_ada[0], b_ada[0])
    mod_p = mod[:b].reshape(b, 1, N_MOD * d)
    mod_s = mod[b:nseq].reshape(db, 1, N_MOD * d)

    attn_p = lambda q, kvp, kvpb, kvw, kvwb, gates: _nsa_prompt_call(q, kvp, kvpb, kvwb, gates, wbig, hb, w2)
    y_p, kvp_p, kvw_p, u_p = _layer(x_prompt, mod_p, attn_p, None, w,
                                    ((1, 512), (1, 256), (1, 256, 32)))

    cache_t = jnp.transpose(cache_kv, (0, 2, 3, 4, 5, 1)).reshape(cache_kv.shape[0], kvp_w, PAGE_SIZE)
    win_t = jnp.transpose(state_win, (0, 2, 3, 4, 5, 1)).reshape(db, kvw_w, wb)
    attn_s = lambda q, kvp, kvpb, kvw, kvwb, gates: _nsa_sample_call(page_table, q, kvp, kvw, win_t, gates,
                                                                      cache_t, wbig, hb, w2)
    sb = min(db, 64)
    y_s, kvp_s, kvw_s, u_s = _layer(x_sample, mod_s, attn_s, state_conv[:, :, 0], w,
                                    ((sb, ds), (min(db, 32), ds), (min(db, 8), ds, ds)))

    kv5 = lambda a: a.reshape(a.shape[0], a.shape[1], 1, KV_HEADS, 4, HEAD_DIM)
    win5 = lambda a: a.reshape(a.shape[0], a.shape[1], 1, KV_HEADS, 2, HEAD_DIM)
    new_win_p = win5(kvw_p[:, t_len - wb:])
    new_win_s = jnp.concatenate([state_win, win5(kvw_s)], axis=1)[:, -wb:]
    new_conv_p = u_p[:, t_len - cw1:][:, :, None, :]
    new_conv_s = jnp.concatenate([state_conv[:, :, 0], u_s], axis=1)[:, -cw1:][:, :, None, :]
    return (y_p, y_s, kv5(kvp_p), kv5(kvp_s), new_win_p, new_win_s, new_conv_p, new_conv_s)
```

```python
import functools

import numpy as np
import jax
import jax.numpy as jnp
from jax import lax
from jax.experimental import pallas as pl
from jax.experimental.pallas import tpu as pltpu

F32 = jnp.float32
BF = jnp.bfloat16

N_HEADS = 16
HEAD_DIM = 64
KV_HEADS = 4
HPG = N_HEADS // KV_HEADS
CMP_LEN = 32
CMP_STRIDE = 16
SEL_BLOCK = 64
N_SEL = 16
WINDOW = 512
PHI_HID = 128
CONV_WIDTH = 31
N_MOD = 9
EPS = 1e-6
NEG = -1e30
FORCE = 1e4
PAGE_SIZE = 128

LANES = 128
KV_LANES = 2 * HEAD_DIM
Q_LANES = HPG * HEAD_DIM
N_GATE = 3 * HPG
N_POS = 6
VMEM_LIMIT = 56 * 1024 * 1024


def _cparams(sem):
    return pltpu.CompilerParams(dimension_semantics=sem, vmem_limit_bytes=VMEM_LIMIT)


def _const_spec(shape):
    nd = len(shape)
    return pl.BlockSpec(shape, lambda *_: (0,) * nd, pipeline_mode=pl.Buffered(1))


def _tok(ref):
    tb, rb, w = ref.shape
    return ref[0] if tb == 1 else ref[...].reshape(tb * rb, w)


def _modrow(ref, rb):
    tb, _, w = ref.shape
    if tb == 1:
        return ref[0]
    return jnp.broadcast_to(ref[...], (tb, rb, w)).reshape(tb * rb, w)


def _put(ref, val):
    tb, rb, w = ref.shape
    if tb == 1:
        ref[0] = val.astype(ref.dtype)
    else:
        ref[...] = val.reshape(tb, rb, w).astype(ref.dtype)


def _rms(x, g):
    return x * lax.rsqrt(jnp.mean(x * x, axis=-1, keepdims=True) + EPS) * g


def _silu(x):
    return x * jax.nn.sigmoid(x)


def _mm(a, b):
    return jnp.dot(a, b, preferred_element_type=F32)


def _mm_t(a, b):
    return lax.dot_general(a, b, (((1,), (1,)), ((), ())), preferred_element_type=F32)


def _tok_spec(tb, rb, w):
    return pl.BlockSpec((tb, rb, w), lambda i, j: (i, j, 0))


def _mod_spec(tb, d, col):
    return pl.BlockSpec((tb, 1, d), lambda i, j: (i, 0, col))


def _mod_kernel(c_ref, w_ref, b_ref, o_ref):
    a = _silu(c_ref[...]).astype(BF)
    o_ref[...] = _mm(a, w_ref[...].astype(BF)) + b_ref[...]


def _mod_call(c, w, b):
    m, d = c.shape
    n = w.shape[1]
    tn = d
    return pl.pallas_call(
        _mod_kernel,
        grid=(n // tn,),
        in_specs=[pl.BlockSpec((m, d), lambda j: (0, 0)),
                  pl.BlockSpec((d, tn), lambda j: (0, j)),
                  pl.BlockSpec((1, tn), lambda j: (0, j))],
        out_specs=pl.BlockSpec((m, tn), lambda j: (0, j)),
        out_shape=jax.ShapeDtypeStruct((m, n), F32),
        compiler_params=_cparams(("arbitrary",)),
    )(c, w, b.reshape(1, n))


def _ffn_kernel(*refs, d_ff, fc, final):
    if final:
        x_ref, sh_ref, sc_ref, gt_ref, g_ref, wi_ref, wo_ref, gf_ref, h_ref, y_ref = refs
    else:
        x_ref, sh_ref, sc_ref, gt_ref, g_ref, wi_ref, wo_ref, h_ref = refs
    rb = x_ref.shape[1]
    x = _tok(x_ref)
    n = (_rms(x, g_ref[...]) * (1.0 + _modrow(sc_ref, rb)) + _modrow(sh_ref, rb)).astype(BF)
    acc = jnp.zeros(x.shape, F32)
    for c in range(d_ff // fc):
        gate = _mm(n, wi_ref[:, c * fc:(c + 1) * fc])
        up = _mm(n, wi_ref[:, d_ff + c * fc:d_ff + (c + 1) * fc])
        acc = acc + _mm((_silu(gate) * up).astype(BF), wo_ref[c * fc:(c + 1) * fc, :])
    h = x + _modrow(gt_ref, rb) * (0.5 * acc)
    _put(h_ref, h)
    if final:
        _put(y_ref, _rms(h, gf_ref[...]))


def _ffn_call(x, mod, col0, g, wi, wo, tb, rb, gf=None):
    nb, r, d = x.shape
    d_ff = wo.shape[0]
    fc = d_ff // 2
    final = gf is not None
    ins = [x, mod, mod, mod, g.reshape(1, d), wi, wo]
    specs = [_tok_spec(tb, rb, d), _mod_spec(tb, d, col0), _mod_spec(tb, d, col0 + 1),
             _mod_spec(tb, d, col0 + 2), _const_spec((1, d)), _const_spec(wi.shape), _const_spec(wo.shape)]
    out_shape = [jax.ShapeDtypeStruct(x.shape, F32)]
    out_specs = [_tok_spec(tb, rb, d)]
    if final:
        ins.append(gf.reshape(1, d))
        specs.append(_const_spec((1, d)))
        out_shape.append(jax.ShapeDtypeStruct(x.shape, F32))
        out_specs.append(_tok_spec(tb, rb, d))
    return pl.pallas_call(
        functools.partial(_ffn_kernel, d_ff=d_ff, fc=fc, final=final),
        grid=(nb // tb, r // rb),
        in_specs=specs, out_specs=out_specs, out_shape=out_shape,
        compiler_params=_cparams(("arbitrary", "arbitrary")),
    )(*ins)


def _proj_kernel(h_ref, sh_ref, sc_ref, g_ref, wq_ref, wkvp_ref, wkvw_ref, wng_ref, wglu_ref, wmg_ref,
                 q_ref, kvp_ref, kvpb_ref, kvw_ref, kvwb_ref, gates_ref, u_ref, sga_ref, sgc_ref):
    tb, rb, d = h_ref.shape
    n = (_rms(_tok(h_ref), g_ref[...]) * (1.0 + _modrow(sc_ref, rb)) + _modrow(sh_ref, rb)).astype(BF)
    _put(q_ref, _mm(n, wq_ref[...]))
    kvp = _mm(n, wkvp_ref[...])
    _put(kvp_ref, kvp)
    _put(kvpb_ref, kvp)
    kvw = _mm(n, wkvw_ref[...])
    _put(kvw_ref, kvw)
    _put(kvwb_ref, kvw)
    ng = jax.nn.sigmoid(_mm(n, wng_ref[...]))
    for g in range(KV_HEADS):
        gates_ref[g] = ng[:, g * N_GATE:(g + 1) * N_GATE].reshape(tb, rb, N_GATE)
    glu = _mm(n, wglu_ref[...])
    c = glu.shape[1] // 2
    _put(u_ref, glu[:, :c] * jax.nn.sigmoid(glu[:, c:]))
    mg = jax.nn.sigmoid(_mm(n, wmg_ref[...]))
    _put(sga_ref, mg[:, :d])
    _put(sgc_ref, mg[:, d:])


def _proj_call(h, mod, g, ws, tb, rb):
    nb, r, d = h.shape
    wq, wkvp, wkvw, wng, wglu, wmg = ws
    c = wglu.shape[1] // 2
    widths = [(wq.shape[1], BF), (wkvp.shape[1], F32), (wkvp.shape[1], BF), (wkvw.shape[1], F32),
              (wkvw.shape[1], BF)]
    out_shape = [jax.ShapeDtypeStruct((nb, r, w), dt) for w, dt in widths]
    out_specs = [_tok_spec(tb, rb, w) for w, _ in widths]
    out_shape.append(jax.ShapeDtypeStruct((KV_HEADS, nb, r, N_GATE), F32))
    out_specs.append(pl.BlockSpec((KV_HEADS, tb, rb, N_GATE), lambda i, j: (0, i, j, 0)))
    for w in (c, d, d):
        out_shape.append(jax.ShapeDtypeStruct((nb, r, w), F32))
        out_specs.append(_tok_spec(tb, rb, w))
    return pl.pallas_call(
        _proj_kernel,
        grid=(nb // tb, r // rb),
        in_specs=[_tok_spec(tb, rb, d), _mod_spec(tb, d, 3), _mod_spec(tb, d, 4), _const_spec((1, d))]
                 + [_const_spec(w.shape) for w in ws],
        out_specs=out_specs, out_shape=out_shape,
        compiler_params=_cparams(("arbitrary", "arbitrary")),
    )(h, mod, mod, g.reshape(1, d), *ws)


CONV_HIST = 32


def _conv_kernel(u_ref, hist_ref, w_ref, b_ref, g_ref, bl_ref, o_ref, sh_ref, *, zero_first, rs):
    tb, rb, c = u_ref.shape
    hr = hist_ref.shape[1]
    seg = CONV_HIST + rb
    hist = hist_ref[...]
    if zero_first:
        hist = jnp.where(pl.program_id(1) > 0, hist, 0.0)
    for s in range(tb):
        if hr < CONV_HIST:
            sh_ref[0, s * seg:s * seg + CONV_HIST - hr, :] = jnp.zeros((CONV_HIST - hr, c), F32)
        sh_ref[0, s * seg + CONV_HIST - hr:s * seg + CONV_HIST, :] = hist[s]
        sh_ref[0, s * seg + CONV_HIST:(s + 1) * seg, :] = u_ref[s]
    ext = sh_ref[0]
    for b in range(1, 8):
        sh_ref[b] = pltpu.roll(ext, tb * seg - b, 0)
    first = CONV_HIST - (CONV_WIDTH - 1)
    w = w_ref[...]
    for s in range(tb):
        for r0 in range(0, rb, rs):
            acc = jnp.broadcast_to(b_ref[...], (rs, c))
            for k in range(CONV_WIDTH):
                a, b = divmod(k + first, 8)
                row = s * seg + r0 + 8 * a
                acc = acc + w[k:k + 1, :] * sh_ref[b, row:row + rs, :]
            xc = acc - jnp.mean(acc, axis=-1, keepdims=True)
            y = xc * lax.rsqrt(jnp.mean(xc * xc, axis=-1, keepdims=True) + EPS) * g_ref[...] + bl_ref[...]
            o_ref[s, r0:r0 + rs, :] = _silu(y).astype(o_ref.dtype)


def _conv_call(u, hist, w, b, g, bl, tb, rb, rs):
    nb, r, c = u.shape
    row = lambda v: v.reshape(1, c)
    if hist is None:
        hist_arr, hr = u, CONV_HIST
        hspec = pl.BlockSpec((tb, hr, c), lambda i, j: (i, jnp.maximum(j * (rb // CONV_HIST) - 1, 0), 0))
    else:
        hist_arr, hr = hist, hist.shape[1]
        hspec = pl.BlockSpec((tb, hr, c), lambda i, j: (i, 0, 0))
    return pl.pallas_call(
        functools.partial(_conv_kernel, zero_first=hist is None, rs=rs),
        grid=(nb // tb, r // rb),
        in_specs=[_tok_spec(tb, rb, c), hspec, _const_spec(w.shape)] + [_const_spec((1, c))] * 3,
        out_specs=_tok_spec(tb, rb, c),
        out_shape=jax.ShapeDtypeStruct(u.shape, BF),
        scratch_shapes=[pltpu.VMEM((8, tb * (CONV_HIST + rb), c), F32)],
        compiler_params=_cparams(("arbitrary", "arbitrary")),
    )(u, hist_arr, w, row(b), row(g), row(bl))


def _merge_kernel(h_ref, attn_ref, conv_ref, sga_ref, sgc_ref, gt_ref, wa_ref, wc_ref, wo_ref, o_ref):
    rb = h_ref.shape[1]
    m = _tok(sga_ref) * _mm(_tok(attn_ref), wa_ref[...]) + _tok(sgc_ref) * _mm(_tok(conv_ref), wc_ref[...])
    _put(o_ref, _tok(h_ref) + _modrow(gt_ref, rb) * _mm(m.astype(BF), wo_ref[...]))


def _merge_call(h, attn, conv, sga, sgc, mod, wa, wc, wo, tb, rb):
    nb, r, d = h.shape
    return pl.pallas_call(
        _merge_kernel,
        grid=(nb // tb, r // rb),
        in_specs=[_tok_spec(tb, rb, d), _tok_spec(tb, rb, attn.shape[2]), _tok_spec(tb, rb, conv.shape[2]),
                  _tok_spec(tb, rb, d), _tok_spec(tb, rb, d), _mod_spec(tb, d, 5),
                  _const_spec(wa.shape), _const_spec(wc.shape), _const_spec(wo.shape)],
        out_specs=_tok_spec(tb, rb, d),
        out_shape=jax.ShapeDtypeStruct(h.shape, F32),
        compiler_params=_cparams(("arbitrary", "arbitrary")),
    )(h, attn, conv, sga, sgc, mod, wa, wc, wo)


def _cmp_weights(w_phi1, w_phi2):
    c = CMP_LEN // CMP_STRIDE
    w = w_phi1.reshape(2, c, CMP_STRIDE, HEAD_DIM, PHI_HID)
    eye = jnp.eye(2, dtype=w.dtype)
    wbig = jnp.einsum('kosdh,kc->skdcoh', w, eye).reshape(CMP_STRIDE * KV_LANES, 2 * c * PHI_HID)
    w2 = jnp.einsum('khd,kc->khcd', w_phi2, eye).reshape(2 * PHI_HID, KV_LANES)
    return wbig.astype(BF), w2.astype(BF)


def _cmp_bias_kernel(pe_ref, wbig_ref, b_ref, o_ref):
    res = _mm(pe_ref[...].astype(BF), wbig_ref[...])
    hs = [b_ref[:, k * PHI_HID:(k + 1) * PHI_HID]
          + res[0:1, 2 * k * PHI_HID:(2 * k + 1) * PHI_HID]
          + res[1:2, (2 * k + 1) * PHI_HID:(2 * k + 2) * PHI_HID] for k in range(2)]
    o_ref[...] = jnp.concatenate(hs, axis=1)


def _cmp_bias_call(pe_cmp, wbig, b_phi1):
    c = CMP_LEN // CMP_STRIDE
    pe = pe_cmp.reshape(2, c, CMP_STRIDE, HEAD_DIM).transpose(1, 2, 0, 3).reshape(c, CMP_STRIDE * KV_LANES)
    pe = jnp.concatenate([pe, jnp.zeros((8 - c, pe.shape[1]), pe.dtype)], axis=0)
    return pl.pallas_call(
        _cmp_bias_kernel,
        out_shape=jax.ShapeDtypeStruct((1, 2 * PHI_HID), F32),
        compiler_params=pltpu.CompilerParams(vmem_limit_bytes=VMEM_LIMIT),
    )(pe, wbig, b_phi1.reshape(1, 2 * PHI_HID))


def _compress(lhs, wbig_ref, hb_ref, w2_ref):
    part = _mm(lhs, wbig_ref[...])
    n = part.shape[0]
    nxt = pltpu.roll(part, n - 1, 0)
    hb = hb_ref[...]
    hs = [hb[:, k * PHI_HID:(k + 1) * PHI_HID]
          + part[:, 2 * k * PHI_HID:(2 * k + 1) * PHI_HID]
          + nxt[:, (2 * k + 1) * PHI_HID:(2 * k + 2) * PHI_HID] for k in range(2)]
    h = jnp.concatenate(hs, axis=1)
    return _mm(_silu(h).astype(BF), w2_ref[...])


def _stack_q(q, r):
    q = q.astype(F32)
    return jnp.concatenate([q[:, h * HEAD_DIM:(h + 1) * HEAD_DIM] for h in range(HPG)], axis=0)


def _row_info(r, g, t0):
    row = lax.broadcasted_iota(jnp.int32, (HPG * r, 1), 0)
    t = t0 + (row & (r - 1))
    head = g * HPG + lax.shift_right_logical(row, r.bit_length() - 1)
    slope = jnp.exp2(-(8.0 / N_HEADS) * (head + 1).astype(F32))
    return t, slope


def _q_aux(slope, width, pos0, selneg=None):
    hi = slope.astype(BF).astype(F32)
    rest = slope - hi
    mid = rest.astype(BF).astype(F32)
    lo = rest - mid
    lane = lax.broadcasted_iota(jnp.int32, (slope.shape[0], width), 1) - pos0
    base = jnp.zeros((slope.shape[0], width), F32) if selneg is None else selneg
    aux = jnp.where((lane == 0) | (lane == 3), hi, base)
    aux = jnp.where((lane == 1) | (lane == 4), mid, aux)
    aux = jnp.where((lane == 2) | (lane == 5), lo, aux)
    return aux.astype(BF)


def _k_aux(pos, blocks, width, pos0):
    pos = np.asarray(pos)
    a = np.zeros((pos.shape[0], width), np.float32)
    if blocks is not None:
        a[np.arange(pos.shape[0]), np.asarray(blocks)] = 1.0
    a[:, pos0:pos0 + 3] = (pos // SEL_BLOCK * SEL_BLOCK)[:, None]
    a[:, pos0 + 3:pos0 + 6] = (pos % SEL_BLOCK)[:, None]
    return a


def _topk_mask(xt, k):
    nb = xt.shape[0]
    rowf = lax.broadcasted_iota(jnp.int32, xt.shape, 0).astype(F32)

    def body(_, carry):
        x, sel = carry
        m = jnp.max(x, axis=0, keepdims=True)
        idx = jnp.min(jnp.where(x == m, rowf, float(nb)), axis=0, keepdims=True)
        pick = rowf == idx
        return jnp.where(pick, -3.0, x), jnp.where(pick, 1.0, sel)

    _, sel = lax.fori_loop(0, k, body, (xt, jnp.zeros_like(xt)))
    return sel


def _select(score, ns):
    nbp = score.shape[1]
    nb8 = -(-ns // 8) * 8
    sel_t = _topk_mask(score.T[0:nb8], N_SEL)
    if nb8 < nbp:
        sel_t = jnp.concatenate([sel_t, jnp.zeros((nbp - nb8, sel_t.shape[1]), F32)], axis=0)
    return sel_t.T


def _cmp_branch(qx, kvc, caux_ref, t, r, n_cmp, ns, m_ref):
    nc = kvc.shape[0]
    nbp = m_ref.shape[1]
    ci = lax.broadcasted_iota(jnp.int32, (1, nc), 1)
    mask = (t >= ci * CMP_STRIDE + (CMP_LEN - 1)) & (ci < n_cmp)
    s = jnp.where(mask, _mm_t(qx, jnp.concatenate([kvc, caux_ref[...]], axis=1)), NEG)
    e = jnp.where(mask, jnp.exp(s - jnp.max(s, axis=-1, keepdims=True)), 0.0)
    p = e / jnp.maximum(jnp.sum(e, axis=-1, keepdims=True), 1e-30)
    o_c = _mm(p.astype(BF), kvc)
    psum = p[0:r] + p[r:2 * r] + p[2 * r:3 * r] + p[3 * r:4 * r]
    hi = psum.astype(BF).astype(F32)
    mid = (psum - hi).astype(BF).astype(F32)
    lo = psum - hi - mid
    res = _mm(jnp.concatenate([hi, mid, lo, jnp.zeros_like(lo)], axis=0).astype(BF), m_ref[...])
    imp = res[0:r] + res[r:2 * r] + res[2 * r:3 * r]
    blk = lax.broadcasted_iota(jnp.int32, (1, nbp), 1)
    cur = lax.shift_right_logical(t[0:r], SEL_BLOCK.bit_length() - 1)
    forced = (blk == 0) | (blk == cur) | (blk == cur - 1)
    score = jnp.where(blk <= cur, imp + jnp.where(forced, FORCE, 0.0), -1.0)
    return o_c, jnp.where(blk < ns, score, -2.0)


def _ones_v(kv):
    return jnp.where(lax.broadcasted_iota(jnp.int32, kv.shape, 1) < HEAD_DIM, jnp.ones_like(kv), kv)


def _gate_col(gt, branch):
    return jnp.concatenate([gt[:, branch * HPG + h:branch * HPG + h + 1] for h in range(HPG)], axis=0)


def _unstack(o, r):
    return jnp.concatenate([o[h * r:(h + 1) * r] for h in range(HPG)], axis=1)


def _imp_matrix(nc, n_cmp, nbp, ns):
    i = np.arange(nc)[:, None]
    j = np.arange(nbp)[None, :]
    r = SEL_BLOCK // CMP_STRIDE
    c = CMP_LEN // CMP_STRIDE
    return jnp.asarray((i >= r * j - (c - 1)) & (i <= r * j + r - 1) & (i < n_cmp) & (j < ns), BF)


def _nsa_prompt_kernel(q_ref, kc_ref, ks_ref, kw_ref, gt_ref, wbig_ref, hb_ref, w2_ref, kaux_ref, caux_ref,
                       m_ref, o_ref, kvc_ref, *, tq, t_len, ck, ns):
    g = pl.program_id(1)
    i = pl.program_id(2)
    n = t_len // CMP_STRIDE
    rows = HPG * tq

    @pl.when(i == 0)
    def _():
        lhs = jnp.concatenate([kc_ref[0, pl.ds(s, n, stride=CMP_STRIDE), :].astype(BF)
                               for s in range(CMP_STRIDE)], axis=1)
        kvc = _compress(lhs, wbig_ref, hb_ref, w2_ref)
        kvc_ref[...] = kvc.astype(BF)

    start = i * tq
    t, slope = _row_info(tq, g, start)
    qs = _stack_q(q_ref[0], tq)
    qk = jnp.concatenate([qs, jnp.zeros_like(qs)], axis=1).astype(BF)
    qx = jnp.concatenate([qk, _q_aux(slope, LANES, ns)], axis=1)

    o_c, score = _cmp_branch(qx, kvc_ref[...], caux_ref, t, tq, n - 1, ns, m_ref)
    sel = jnp.concatenate([_select(score[j:j + LANES], ns) for j in range(0, tq, LANES)], axis=0)

    lo = pl.multiple_of(jnp.maximum(start - WINDOW, 0), tq)
    kw = kw_ref[0, pl.ds(lo, WINDOW + tq), :]
    s = _mm_t(qx, jnp.concatenate([kw, kaux_ref[pl.ds(lo, WINDOW + tq), :]], axis=1))
    dist = t - (lo + lax.broadcasted_iota(jnp.int32, (1, WINDOW + tq), 1))
    s = jnp.where((dist >= 0) & (dist <= WINDOW), s, NEG)
    e = jnp.exp(s - jnp.max(s, axis=-1, keepdims=True)).astype(BF)
    o_w = _mm(e, _ones_v(kw))
    o_w = o_w / o_w[:, 0:1]

    selneg = jnp.concatenate([(sel - 1.0) * (-NEG)] * HPG, axis=0)
    qx = jnp.concatenate([qk, _q_aux(slope, LANES, ns, selneg)], axis=1)

    def chunk(c, carry, causal):
        m, acc = carry
        k0 = pl.multiple_of(c * ck, ck)
        kv = ks_ref[0, pl.ds(k0, ck), :]
        s = _mm_t(qx, jnp.concatenate([kv, kaux_ref[pl.ds(k0, ck), :]], axis=1))
        if causal:
            s = jnp.where(t >= k0 + lax.broadcasted_iota(jnp.int32, (1, ck), 1), s, NEG)
        m_new = jnp.maximum(m, jnp.max(s, axis=-1, keepdims=True))
        e = jnp.exp(s - m_new).astype(BF)
        acc = jnp.exp(m - m_new) * acc + _mm(e, _ones_v(kv))
        return m_new, acc

    init = (jnp.full((rows, 1), NEG, F32), jnp.zeros((rows, KV_LANES), F32))
    last = (start + tq - 1) // ck
    carry = lax.fori_loop(0, last, lambda c, cr: chunk(c, cr, False), init)
    _, acc = chunk(last, carry, True)
    o_s = acc / acc[:, 0:1]

    gt = gt_ref[0, 0]
    o = _gate_col(gt, 0) * o_c + _gate_col(gt, 1) * o_s + _gate_col(gt, 2) * o_w
    o_ref[0] = _unstack(o[:, HEAD_DIM:], tq).astype(o_ref.dtype)


def _nsa_prompt_call(q, kvp, kvpb, kvwb, gates, wbig, hb, w2, tq=256, ck=512):
    b, t_len, _ = q.shape
    nc = t_len // CMP_STRIDE
    ns = t_len // SEL_BLOCK
    assert t_len % ck == 0 and t_len >= WINDOW + tq and N_SEL <= ns <= LANES - N_POS and tq % LANES == 0
    pos = np.arange(t_len)
    kaux = jnp.asarray(_k_aux(pos, pos // SEL_BLOCK, LANES, ns), BF)
    caux = jnp.asarray(_k_aux(np.arange(nc) * CMP_STRIDE + CMP_LEN - 1, None, LANES, ns), BF)
    m_mat = _imp_matrix(nc, nc - 1, LANES, ns)
    return pl.pallas_call(
        functools.partial(_nsa_prompt_kernel, tq=tq, t_len=t_len, ck=ck, ns=ns),
        grid=(b, KV_HEADS, t_len // tq),
        in_specs=[pl.BlockSpec((1, tq, Q_LANES), lambda b, g, i: (b, i, g)),
                  pl.BlockSpec((1, t_len, KV_LANES), lambda b, g, i: (b, 0, 2 * g)),
                  pl.BlockSpec((1, t_len, KV_LANES), lambda b, g, i: (b, 0, 2 * g + 1)),
                  pl.BlockSpec((1, t_len, KV_LANES), lambda b, g, i: (b, 0, g)),
                  pl.BlockSpec((1, 1, tq, N_GATE), lambda b, g, i: (g, b, i, 0)),
                  _const_spec(wbig.shape), _const_spec(hb.shape), _const_spec(w2.shape),
                  _const_spec(kaux.shape), _const_spec(caux.shape), _const_spec(m_mat.shape)],
        out_specs=pl.BlockSpec((1, tq, Q_LANES), lambda b, g, i: (b, i, g)),
        out_shape=jax.ShapeDtypeStruct(q.shape, BF),
        scratch_shapes=[pltpu.VMEM((nc, KV_LANES), BF)],
        compiler_params=_cparams(("arbitrary", "arbitrary", "arbitrary")),
    )(q, kvp, kvpb, kvwb, gates, wbig, hb, w2, kaux, caux, m_mat)


NEW_PAD = 128


def _page_stream(pt_ref, cache_ref, sem, row0, n_pages, dst_of):
    b = pl.program_id(0)
    g = pl.program_id(1)
    step = b * KV_HEADS + g
    total = pl.num_programs(0) * KV_HEADS
    slot = step % 2

    def copy(bb, gg, p, sl):
        rows = pl.ds(pl.multiple_of(gg * 2 * KV_LANES + row0, KV_LANES), KV_LANES)
        return pltpu.make_async_copy(cache_ref.at[pt_ref[bb * n_pages + p], rows, :], dst_of(sl, p), sem.at[sl])

    def fetch(st, sl):
        bb = st // KV_HEADS
        gg = st % KV_HEADS

        def body(p, carry):
            copy(bb, gg, p, sl).start()
            return carry

        lax.fori_loop(0, n_pages, body, 0, unroll=4)

    @pl.when(step == 0)
    def _():
        fetch(step, slot)

    @pl.when(step + 1 < total)
    def _():
        fetch(step + 1, 1 - slot)

    def wait_body(p, carry):
        copy(b, g, p, slot).wait()
        return carry

    lax.fori_loop(0, n_pages, wait_body, 0, unroll=4)
    return slot


def _sample_cmp_kernel(pt_ref, q_ref, cache_ref, perm_ref, wbig_ref, hb_ref, w2_ref, caux_ref, m_ref,
                       oc_ref, score_ref, buf_ref, lhs_ref, sem, *, past, ds, ns):
    g = pl.program_id(1)
    n_pages = past // PAGE_SIZE
    slot = _page_stream(pt_ref, cache_ref, sem, 0, n_pages, lambda sl, p: buf_ref.at[sl, p])

    cpp = PAGE_SIZE // CMP_STRIDE
    perm = perm_ref[...]

    def regroup(i, carry):
        res = [_mm_t(perm, buf_ref[slot, 2 * i + j].astype(BF)) for j in range(2)]
        r0 = pl.multiple_of(i * 2 * cpp, 2 * cpp)
        for s in range(CMP_STRIDE):
            rows = jnp.concatenate([r[s * cpp:(s + 1) * cpp] for r in res], axis=0)
            lhs_ref[pl.ds(r0, 2 * cpp), s * KV_LANES:(s + 1) * KV_LANES] = rows.astype(BF)
        return carry

    lax.fori_loop(0, n_pages // 2, regroup, 0, unroll=True)
    n = past // CMP_STRIDE
    kvc = _compress(lhs_ref[...], wbig_ref, hb_ref, w2_ref).astype(BF)

    t, slope = _row_info(ds, g, past)
    qs = _stack_q(q_ref[0], ds)
    qx = jnp.concatenate([qs, jnp.zeros_like(qs)], axis=1).astype(BF)
    qx = jnp.concatenate([qx, _q_aux(slope, LANES, 0)], axis=1)
    o_c, score = _cmp_branch(qx, kvc, caux_ref, t, ds, n - 1, ns, m_ref)
    oc_ref[0, 0] = o_c
    score_ref[0, 0] = score


def _topk_kernel(score_ref, sel_ref, *, ns):
    sel_ref[...] = _select(score_ref[...], ns)


def _sample_sel_kernel(pt_ref, q_ref, kvpn_ref, kvwn_ref, win_ref, gt_ref, oc_ref, sel_ref, cache_ref, eaux_ref,
                       o_ref, buf_ref, sem, *, past, ds, wb, ns):
    g = pl.program_id(1)
    n_pages = past // PAGE_SIZE
    slot = _page_stream(pt_ref, cache_ref, sem, KV_LANES, n_pages,
                        lambda sl, p: buf_ref.at[sl, :, pl.ds(pl.multiple_of(p * PAGE_SIZE, PAGE_SIZE), PAGE_SIZE)])

    t, slope = _row_info(ds, g, past)
    qs = _stack_q(q_ref[0], ds).astype(BF)
    zpad = jnp.zeros((NEW_PAD - ds, KV_LANES), F32)
    lane = lax.broadcasted_iota(jnp.int32, (1, NEW_PAD), 1)

    def new_t(rows):
        x = jnp.concatenate([rows, zpad], axis=0).T.astype(BF)
        return x[0:HEAD_DIM], x[HEAD_DIM:KV_LANES]

    naux = eaux_ref.shape[0]
    selneg = jnp.concatenate([(sel_ref[0, 0][:, 0:naux] - 1.0) * (-NEG)] * HPG, axis=0)
    qa = _q_aux(slope, naux, ns, selneg)
    kn, vn = new_t(kvpn_ref[0][:, KV_LANES:])
    kt = buf_ref[slot, 0:HEAD_DIM, :].astype(BF)
    vt = buf_ref[slot, HEAD_DIM:KV_LANES, :].astype(BF)
    s_p = _mm(qs, kt) + _mm(qa, eaux_ref[:, 0:past])
    s_n = _mm(qs, kn) + _mm(qa, eaux_ref[:, past:past + NEW_PAD])
    s_n = jnp.where(t >= past + lane, s_n, NEG)
    m = jnp.maximum(jnp.max(s_p, axis=-1, keepdims=True), jnp.max(s_n, axis=-1, keepdims=True))
    e_p = jnp.exp(s_p - m)
    e_n = jnp.exp(s_n - m)
    l = jnp.sum(e_p, axis=-1, keepdims=True) + jnp.sum(e_n, axis=-1, keepdims=True)
    o_s = (_mm_t(e_p.astype(BF), vt) + _mm_t(e_n.astype(BF), vn)) / l

    kn, vn = new_t(kvwn_ref[0])
    kt = win_ref[0, 0:HEAD_DIM, :].astype(BF)
    vt = win_ref[0, HEAD_DIM:KV_LANES, :].astype(BF)

    def band(s, kpos):
        dist = t - kpos
        return jnp.where((dist >= 0) & (dist <= WINDOW) & (kpos >= 0), s - slope * dist.astype(F32), NEG)

    s_p = band(_mm(qs, kt), (past - wb) + lax.broadcasted_iota(jnp.int32, (1, wb), 1))
    s_n = band(_mm(qs, kn), past + lane)
    m = jnp.maximum(jnp.max(s_p, axis=-1, keepdims=True), jnp.max(s_n, axis=-1, keepdims=True))
    e_p = jnp.exp(s_p - m)
    e_n = jnp.exp(s_n - m)
    l = jnp.sum(e_p, axis=-1, keepdims=True) + jnp.sum(e_n, axis=-1, keepdims=True)
    o_w = (_mm_t(e_p.astype(BF), vt) + _mm_t(e_n.astype(BF), vn)) / l

    gt = gt_ref[0, 0]
    o = _gate_col(gt, 0) * oc_ref[0, 0][:, HEAD_DIM:] + _gate_col(gt, 1) * o_s + _gate_col(gt, 2) * o_w
    o_ref[0] = _unstack(o, ds).astype(o_ref.dtype)


def _nsa_sample_call(page_table, q, kvp, kvw, win_t, gates, cache_t, wbig, hb, w2):
    db, ds, _ = q.shape
    n_pages = page_table.shape[1]
    past = n_pages * PAGE_SIZE
    wb = win_t.shape[2]
    assert ds < CMP_STRIDE and ds & (ds - 1) == 0 and ds % 8 == 0
    nc = past // CMP_STRIDE
    ns = -(-(past + ds) // SEL_BLOCK)
    nbp = -(-(ns + N_POS) // LANES) * LANES
    rows = HPG * ds
    caux = jnp.asarray(_k_aux(np.arange(nc) * CMP_STRIDE + CMP_LEN - 1, None, LANES, 0), BF)
    m_mat = _imp_matrix(nc, nc - 1, nbp, ns)
    pos = np.arange(past + NEW_PAD)
    naux = -(-(ns + N_POS) // 16) * 16
    eaux = jnp.asarray(_k_aux(pos, np.minimum(pos // SEL_BLOCK, ns), naux, ns).T, BF)
    key = np.arange(PAGE_SIZE)
    cpp = PAGE_SIZE // CMP_STRIDE
    perm = jnp.asarray((key % CMP_STRIDE * cpp + key // CMP_STRIDE)[None, :] == key[:, None], BF)
    qspec = pl.BlockSpec((1, ds, Q_LANES), lambda b, g, pt: (b, 0, g))
    bg_spec = lambda r, w: pl.BlockSpec((1, 1, r, w), lambda b, g, pt: (b, g, 0, 0))

    o_c, score = pl.pallas_call(
        functools.partial(_sample_cmp_kernel, past=past, ds=ds, ns=ns),
        grid_spec=pltpu.PrefetchScalarGridSpec(
            num_scalar_prefetch=1, grid=(db, KV_HEADS),
            in_specs=[qspec, pl.BlockSpec(memory_space=pl.ANY), _const_spec(perm.shape),
                      _const_spec(wbig.shape), _const_spec(hb.shape), _const_spec(w2.shape),
                      _const_spec(caux.shape), _const_spec(m_mat.shape)],
            out_specs=[bg_spec(rows, KV_LANES), bg_spec(ds, nbp)],
            scratch_shapes=[pltpu.VMEM((2, n_pages, KV_LANES, PAGE_SIZE), F32),
                            pltpu.VMEM((nc, CMP_STRIDE * KV_LANES), BF),
                            pltpu.SemaphoreType.DMA((2,))]),
        out_shape=[jax.ShapeDtypeStruct((db, KV_HEADS, rows, KV_LANES), F32),
                   jax.ShapeDtypeStruct((db, KV_HEADS, ds, nbp), F32)],
        compiler_params=_cparams(("arbitrary", "arbitrary")),
    )(page_table.reshape(-1), q, cache_t, perm, wbig, hb, w2, caux, m_mat)

    nq = db * KV_HEADS * ds
    nq_pad = -(-nq // LANES) * LANES
    score2 = jnp.pad(score.reshape(nq, nbp), ((0, nq_pad - nq), (0, 0)), constant_values=-2.0)
    sel = pl.pallas_call(
        functools.partial(_topk_kernel, ns=ns),
        grid=(nq_pad // LANES,),
        in_specs=[pl.BlockSpec((LANES, nbp), lambda i: (i, 0))],
        out_specs=pl.BlockSpec((LANES, nbp), lambda i: (i, 0)),
        out_shape=jax.ShapeDtypeStruct((nq_pad, nbp), F32),
        compiler_params=_cparams(("arbitrary",)),
    )(score2)[:nq].reshape(db, KV_HEADS, ds, nbp)

    return pl.pallas_call(
        functools.partial(_sample_sel_kernel, past=past, ds=ds, wb=wb, ns=ns),
        grid_spec=pltpu.PrefetchScalarGridSpec(
            num_scalar_prefetch=1, grid=(db, KV_HEADS),
            in_specs=[qspec,
                      pl.BlockSpec((1, ds, 2 * KV_LANES), lambda b, g, pt: (b, 0, g)),
                      pl.BlockSpec((1, ds, KV_LANES), lambda b, g, pt: (b, 0, g)),
                      pl.BlockSpec((1, KV_LANES, wb), lambda b, g, pt: (b, g, 0)),
                      pl.BlockSpec((1, 1, ds, N_GATE), lambda b, g, pt: (g, b, 0, 0)),
                      bg_spec(rows, KV_LANES), bg_spec(ds, nbp),
                      pl.BlockSpec(memory_space=pl.ANY), _const_spec(eaux.shape)],
            out_specs=qspec,
            scratch_shapes=[pltpu.VMEM((2, KV_LANES, past), F32), pltpu.SemaphoreType.DMA((2,))]),
        out_shape=jax.ShapeDtypeStruct(q.shape, BF),
        compiler_params=_cparams(("arbitrary", "arbitrary")),
    )(page_table.reshape(-1), q, kvp, kvw, win_t, gates, o_c, sel, cache_t, eaux)


def _layer(x, mod, attn_fn, conv_hist, w, tiles):
    (tb_f, rb_f), (tb_p, rb_p), (tb_c, rb_c, rs_c) = tiles
    h1 = _ffn_call(x, mod, 0, w['g1'], w['ff1_in'], w['ff1_out'], tb_f, rb_f)[0]
    q, kvp, kvpb, kvw, kvwb, gates, u, sga, sgc = _proj_call(h1, mod, w['g2'], w['proj'], tb_p, rb_p)
    conv = _conv_call(u, conv_hist, w['w_dw'], w['b_dw'], w['g_cln'], w['b_cln'], tb_c, rb_c, rs_c)
    attn = attn_fn(q, kvp, kvpb, kvw, kvwb, gates)
    h2 = _merge_call(h1, attn, conv, sga, sgc, mod, w['br_attn'], w['br_conv'], w['out'], tb_f, rb_f)
    _, y = _ffn_call(h2, mod, 6, w['g3'], w['ff2_in'], w['ff2_out'], tb_f, rb_f, gf=w['g_final'])
    return y, kvp, kvw, u


def kernel(x_prompt, x_sample, c_prompt, c_sample, cache_kv, state_win, state_conv, page_table, w_ada, b_ada,
           g_norm1, g_norm2, g_norm3, g_final, w_ff1_in, w_ff1_out, w_ff2_in, w_ff2_out, w_in, pe_cmp, w_phi1,
           b_phi1, w_phi2, w_dw, b_dw, g_conv_ln, b_conv_ln, w_br_attn, w_br_conv, w_out):
    assert w_ada.shape[0] == 1 and cache_kv.shape[2] == 1, "single layer"
    b, t_len, d = x_prompt.shape
    db, ds, _ = x_sample.shape
    wb = state_win.shape[1]
    cw1 = CONV_WIDTH - 1
    q_w = N_HEADS * HEAD_DIM
    kvp_w = KV_HEADS * 4 * HEAD_DIM
    kvw_w = KV_HEADS * 2 * HEAD_DIM
    ng_w = 3 * N_HEADS
    c_dim = w_dw.shape[2]
    assert t_len >= wb and t_len >= cw1

    wi = w_in[0]
    o0, o1, o2, o3, o4 = np.cumsum([q_w, kvp_w, kvw_w, ng_w, 2 * c_dim])
    perm = np.array([r * N_HEADS + g * HPG + h for g in range(KV_HEADS) for r in range(3) for h in range(HPG)])
    w_ng = jnp.pad(wi[:, o2:o3][:, perm], ((0, 0), (0, LANES - ng_w)))
    proj = tuple(v.astype(BF) for v in (wi[:, :o0] * HEAD_DIM ** -0.5, wi[:, o0:o1], wi[:, o1:o2], w_ng,
                                        wi[:, o3:o4], wi[:, o4:]))
    w = dict(g1=g_norm1[0], g2=g_norm2[0], g3=g_norm3[0], g_final=g_final,
             ff1_in=w_ff1_in[0].astype(BF), ff1_out=w_ff1_out[0].astype(BF),
             ff2_in=w_ff2_in[0].astype(BF), ff2_out=w_ff2_out[0].astype(BF), proj=proj,
             w_dw=w_dw[0], b_dw=b_dw[0], g_cln=g_conv_ln[0], b_cln=b_conv_ln[0],
             br_attn=w_br_attn[0].astype(BF), br_conv=w_br_conv[0].astype(BF), out=w_out[0].astype(BF))
    wbig, w2 = _cmp_weights(w_phi1[0], w_phi2[0])
    hb = _cmp_bias_call(pe_cmp[0], wbig, b_phi1[0])

    nseq = b + db
    pad = -nseq % 8
    c_all = jnp.concatenate([c_prompt, c_sample, jnp.zeros((pad, d), F32)], axis=0)
    mod = _mod_call(c_all, w_ada[0], b_ada[0])
    mod_p = mod[:b].reshape(b, 1, N_MOD * d)
    mod_s = mod[b:nseq].reshape(db, 1, N_MOD * d)

    attn_p = lambda q, kvp, kvpb, kvw, kvwb, gates: _nsa_prompt_call(q, kvp, kvpb, kvwb, gates, wbig, hb, w2)
    y_p, kvp_p, kvw_p, u_p = _layer(x_prompt, mod_p, attn_p, None, w,
                                    ((1, 512), (1, 256), (1, 256, 32)))

    cache_t = jnp.transpose(cache_kv, (0, 2, 3, 4, 5, 1)).reshape(cache_kv.shape[0], kvp_w, PAGE_SIZE)
    win_t = jnp.transpose(state_win, (0, 2, 3, 4, 5, 1)).reshape(db, kvw_w, wb)
    attn_s = lambda q, kvp, kvpb, kvw, kvwb, gates: _nsa_sample_call(page_table, q, kvp, kvw, win_t, gates,
                                                                      cache_t, wbig, hb, w2)
    sb = min(db, 64)
    y_s, kvp_s, kvw_s, u_s = _layer(x_sample, mod_s, attn_s, state_conv[:, :, 0], w,
                                    ((sb, ds), (min(db, 32), ds), (min(db, 8), ds, ds)))

    kv5 = lambda a: a.reshape(a.shape[0], a.shape[1], 1, KV_HEADS, 4, HEAD_DIM)
    win5 = lambda a: a.reshape(a.shape[0], a.shape[1], 1, KV_HEADS, 2, HEAD_DIM)
    new_win_p = win5(kvw_p[:, t_len - wb:])
    new_win_s = jnp.concatenate([state_win, win5(kvw_s)], axis=1)[:, -wb:]
    new_conv_p = u_p[:, t_len - cw1:][:, :, None, :]
    new_conv_s = jnp.concatenate([state_conv[:, :, 0], u_s], axis=1)[:, -cw1:][:, :, None, :]
    return (y_p, y_s, kv5(kvp_p), kv5(kvp_s), new_win_p, new_win_s, new_conv_p, new_conv_s)
```
